```python
import jax, jax.numpy as jnp
from jax import lax
import numpy as np

D_MODEL = 4096
BATCH = 1
SEQ = 8192
DEPTH = 1
DEC_BATCH = 128
DEC_SEQ = 1
PAST_LEN = 2048
PAGE_SIZE = 128

HEAD_DIM = 128
HEADS_PER_GROUP = D_MODEL // 512
ATTN_GROUPS = ((128, 1), (512, 4), (2048, 16))
N_GROUPS = len(ATTN_GROUPS)
QKV_WIDTH = N_GROUPS * HEADS_PER_GROUP * HEAD_DIM
ATTN_OUT = HEADS_PER_GROUP * HEAD_DIM
BAND_BLOCK = 128
POOL_WINDOWS = (2, 4, 8, 16)
N_POOL_GROUPS = len(POOL_WINDOWS)
POOL_WIDTH = D_MODEL // 2
POOL_GROUP = POOL_WIDTH // N_POOL_GROUPS
POOL_BUF = max(POOL_WINDOWS) - 1
IN_SPLITS = (QKV_WIDTH, 2 * QKV_WIDTH, 3 * QKV_WIDTH, 3 * QKV_WIDTH + POOL_WIDTH,
             3 * QKV_WIDTH + POOL_WIDTH + D_MODEL)
IN_WIDTH = 3 * QKV_WIDTH + POOL_WIDTH + 2 * D_MODEL
N_EXPERTS = 32
TOP_K = 4
D_FF = D_MODEL
SWIGLU_LIMIT = 7.0
SWIGLU_ALPHA = 1.702
EXPERT_BLOCK = 128
RMS_EPS = 1e-5
NEG_INF = -1e30

kernel_name = "hybrid_dilated_attn_pool_moe_step"


def rms_norm(x, gain):
    xf = x.astype(jnp.float32)
    y = xf * lax.rsqrt(jnp.mean(xf * xf, axis=-1, keepdims=True) + RMS_EPS)
    return (y * gain.astype(jnp.float32)).astype(x.dtype)


def masked_softmax(s, valid):
    s = jnp.where(valid, s, NEG_INF)
    m = jnp.max(s, axis=-1, keepdims=True)
    p = jnp.exp(s - m)
    l = jnp.sum(p, axis=-1, keepdims=True)
    return p / l, (m + jnp.log(l))[..., 0]


def mixer_inputs(x, norm_mix, w_in, q_norm, k_norm):
    b, t, _ = x.shape
    h = rms_norm(x, norm_mix)
    z = jnp.einsum('btd,dc->btc', h, w_in)
    q, k, v, u, gate_a, gate_b = jnp.split(z, IN_SPLITS, axis=-1)
    shape = (b, t, N_GROUPS, HEADS_PER_GROUP, HEAD_DIM)
    q = rms_norm(q.reshape(shape), q_norm[:, None, :])
    k = rms_norm(k.reshape(shape), k_norm[:, None, :])
    return q, k, v.reshape(shape), u, gate_a, gate_b


def dilated_attn_prompt(q, k, v, window, dil):
    b, t, h, hd = q.shape
    span = dil * BAND_BLOCK
    t_pad = -(-t // span) * span
    pad = ((0, 0), (0, t_pad - t), (0, 0), (0, 0))
    nb = t_pad // span
    qb, kb, vb = (jnp.pad(a, pad).reshape(b, nb, BAND_BLOCK, dil, h, hd) for a in (q, k, v))

    def with_prev(a):
        prev = jnp.concatenate([jnp.zeros_like(a[:, :1]), a[:, :-1]], axis=1)
        return jnp.concatenate([prev, a], axis=2)

    kc, vc = with_prev(kb), with_prev(vb)
    s = jnp.einsum('bnqrhd,bnkrhd->bnrhqk', qb, kc,
                   preferred_element_type=jnp.float32) * (HEAD_DIM ** -0.5)
    qi = jnp.arange(BAND_BLOCK)[:, None] + BAND_BLOCK
    ki = jnp.arange(2 * BAND_BLOCK)[None, :]
    dist = qi - ki
    band = (dist >= 0) & (dist <= window // dil)
    has_prev = (jnp.arange(nb)[:, None, None] > 0) | (ki[None] >= BAND_BLOCK)
    valid = (band[None] & has_prev)[None, :, None, None]
    p, lse = masked_softmax(s, valid)
    o = jnp.einsum('bnrhqk,bnkrhd->bnqrhd', p, vc.astype(jnp.float32))
    o = o.reshape(b, t_pad, h, hd)[:, :t]
    lse = jnp.transpose(lse, (0, 1, 4, 2, 3)).reshape(b, t_pad, h)[:, :t]
    return o, lse


def dilated_attn_sample(q, k_new, v_new, kv_buf, window, dil):
    n = q.shape[1]
    L = kv_buf.shape[2]
    kc = jnp.concatenate([kv_buf[:, 0], k_new], axis=1)
    vc = jnp.concatenate([kv_buf[:, 1], v_new], axis=1)
    n_keys = window // dil + 1
    idx = L + jnp.arange(n)[:, None] - dil * jnp.arange(n_keys)[None, :]
    valid = idx >= 0
    idx = jnp.clip(idx, 0, L + n - 1)
    kg, vg = kc[:, idx], vc[:, idx]
    s = jnp.einsum('bqhd,bqkhd->bhqk', q, kg,
                   preferred_element_type=jnp.float32) * (HEAD_DIM ** -0.5)
    p, lse = masked_softmax(s, valid[None, None])
    o = jnp.einsum('bhqk,bqkhd->bqhd', p, vg.astype(jnp.float32))
    return o, jnp.transpose(lse, (0, 2, 1))


def combine_groups(outs, lses):
    o = jnp.stack(outs)
    w = jax.nn.softmax(jnp.stack(lses), axis=0)
    return jnp.einsum('gbth,gbthd->bthd', w, o)


def pool_mixer(u_ext, start_pos, w_pool_group, pool_scale):
    b, tot, c = u_ext.shape
    n = tot - POOL_BUF
    uf = u_ext.astype(jnp.float32)
    cs = jnp.concatenate([jnp.zeros((b, 1, c), jnp.float32), jnp.cumsum(uf, axis=1)], axis=1)
    end = cs[:, POOL_BUF + 1:]
    u_new = uf[:, POOL_BUF:]
    pos = start_pos + jnp.arange(n)
    outs = []
    for g, P in enumerate(POOL_WINDOWS):
        ch = slice(g * POOL_GROUP, (g + 1) * POOL_GROUP)
        win_sum = end[..., ch] - cs[:, POOL_BUF + 1 - P: POOL_BUF + 1 - P + n, ch]
        count = jnp.minimum(pos + 1, P).astype(jnp.float32)[None, :, None]
        outs.append(win_sum / count - u_new[..., ch])
    d = jnp.stack(outs, axis=2)
    y = jnp.einsum('bngc,gce->bnge', d, w_pool_group.astype(jnp.float32)).reshape(b, n, c)
    return (y * pool_scale).astype(u_ext.dtype)


def token_mixer(x, kv_bufs, pool_buf, start_pos, norm_mix, w_in, q_norm, k_norm, w_pool_group,
                pool_scale, w_branch_attn, w_branch_pool, w_out):
    b, t, _ = x.shape
    q, k, v, u, gate_a, gate_b = mixer_inputs(x, norm_mix, w_in, q_norm, k_norm)
    outs, lses, kv_new = [], [], []
    for g, (window, dil) in enumerate(ATTN_GROUPS):
        qg, kg, vg = q[:, :, g], k[:, :, g], v[:, :, g]
        if kv_bufs is None:
            o, l = dilated_attn_prompt(qg, kg, vg, window, dil)
            keep = min(window, t)
            kv_new.append(jnp.stack([kg[:, t - keep:], vg[:, t - keep:]], axis=1))
        else:
            o, l = dilated_attn_sample(qg, kg, vg, kv_bufs[g], window, dil)
            kv_new.append(jnp.stack([kg, vg], axis=1))
        outs.append(o)
        lses.append(l)
    o_attn = combine_groups(outs, lses).reshape(b, t, ATTN_OUT).astype(x.dtype)
    if pool_buf is None:
        ctx = jnp.zeros((b, POOL_BUF, POOL_WIDTH), u.dtype)
        pool_new = u[:, t - min(POOL_BUF, t):]
    else:
        ctx = pool_buf.astype(u.dtype)
        pool_new = u
    pool_out = pool_mixer(jnp.concatenate([ctx, u], axis=1), start_pos, w_pool_group, pool_scale)
    a = jnp.einsum('btc,cd->btd', o_attn, w_branch_attn)
    p = jnp.einsum('btc,cd->btd', pool_out, w_branch_pool)
    mixed = jax.nn.sigmoid(gate_a) * a + jax.nn.sigmoid(gate_b) * p
    return x + jnp.einsum('btd,de->bte', mixed, w_out), kv_new, pool_new


def moe_ffn(x, layer, norm_ffn, w_router, b_router, w_gate_up, b_gate_up, w_down, b_down):
    b, t, d = x.shape
    n_tok = b * t
    h = rms_norm(x, norm_ffn).reshape(n_tok, d)
    logits = (jnp.einsum('td,de->te', h, w_router) + b_router).astype(jnp.float32)
    top_logit, top_e = lax.top_k(logits, TOP_K)
    top_w = jax.nn.softmax(top_logit, axis=-1)
    n_rows = n_tok * TOP_K
    flat_e = top_e.reshape(-1)
    order = jnp.argsort(flat_e)
    sorted_e = flat_e[order]
    sorted_tok = order // TOP_K
    sorted_w = top_w.reshape(-1)[order]
    counts = jnp.bincount(flat_e, length=N_EXPERTS)
    padded = (counts + EXPERT_BLOCK - 1) // EXPERT_BLOCK * EXPERT_BLOCK
    start = jnp.cumsum(counts) - counts
    pend = jnp.cumsum(padded)
    pstart = pend - padded
    dest = pstart[sorted_e] + jnp.arange(n_rows) - start[sorted_e]
    n_blocks = (n_rows + N_EXPERTS * (EXPERT_BLOCK - 1) + EXPERT_BLOCK - 1) // EXPERT_BLOCK
    row_tok = jnp.full((n_blocks * EXPERT_BLOCK,), n_tok, jnp.int32).at[dest].set(sorted_tok)
    row_w = jnp.zeros((n_blocks * EXPERT_BLOCK,), jnp.float32).at[dest].set(sorted_w)
    block_e = jnp.searchsorted(pend, jnp.arange(n_blocks) * EXPERT_BLOCK, side='right')
    block_e = jnp.minimum(block_e, N_EXPERTS - 1)
    h_ext = jnp.concatenate([h, jnp.zeros((1, d), h.dtype)], axis=0)
    xb = h_ext[row_tok].reshape(n_blocks, EXPERT_BLOCK, d)

    def expert_block(args):
        xe, e = args
        gu = jnp.einsum('nd,df->nf', xe, w_gate_up[layer, e]) + b_gate_up[layer, e]
        gate = jnp.minimum(gu[:, :D_FF], SWIGLU_LIMIT)
        up = jnp.clip(gu[:, D_FF:], -SWIGLU_LIMIT, SWIGLU_LIMIT)
        act = gate * jax.nn.sigmoid(SWIGLU_ALPHA * gate) * (up + 1.0)
        return jnp.einsum('nf,fd->nd', act, w_down[layer, e]) + b_down[layer, e]

    yb = lax.map(expert_block, (xb, block_e))
    y = jnp.zeros((n_tok + 1, d), jnp.float32).at[row_tok].add(
        yb.reshape(-1, d).astype(jnp.float32) * row_w[:, None])
    return x + y[:n_tok].reshape(b, t, d).astype(x.dtype)


def setup_inputs(seed: int = 0) -> dict:
    key = jax.random.key(seed)
    ks = jax.random.split(key, 24)
    f32 = jnp.float32

    def normal(k, shape, scale):
        return jax.random.normal(k, shape, f32) * scale

    def gain(k, shape):
        return 1.0 + 0.05 * jax.random.normal(k, shape, f32)

    def kv_cache(k, window):
        return normal(k, (DEPTH, DEC_BATCH, 2, min(window, PAST_LEN), HEADS_PER_GROUP, HEAD_DIM), 1.0)

    return {
        "x_prompt": normal(ks[0], (BATCH, SEQ, D_MODEL), 1.0),
        "x_sample": normal(ks[1], (DEC_BATCH, DEC_SEQ, D_MODEL), 1.0),
        "cache_kv_w128": kv_cache(ks[2], ATTN_GROUPS[0][0]),
        "cache_kv_w512": kv_cache(ks[3], ATTN_GROUPS[1][0]),
        "cache_kv_w2048": kv_cache(ks[4], ATTN_GROUPS[2][0]),
        "state_pool": normal(ks[5], (DEPTH, DEC_BATCH, POOL_BUF, POOL_WIDTH), 1.0),
        "norm_mix": gain(ks[6], (DEPTH, D_MODEL)),
        "w_in": normal(ks[7], (DEPTH, D_MODEL, IN_WIDTH), D_MODEL ** -0.5),
        "q_norm": gain(ks[8], (DEPTH, N_GROUPS, HEAD_DIM)),
        "k_norm": gain(ks[9], (DEPTH, N_GROUPS, HEAD_DIM)),
        "w_pool_group": normal(ks[10], (DEPTH, N_POOL_GROUPS, POOL_GROUP, POOL_GROUP), POOL_GROUP ** -0.5),
        "pool_scale": gain(ks[11], (DEPTH, POOL_WIDTH)),
        "w_branch_attn": normal(ks[12], (DEPTH, ATTN_OUT, D_MODEL), ATTN_OUT ** -0.5),
        "w_branch_pool": normal(ks[13], (DEPTH, POOL_WIDTH, D_MODEL), POOL_WIDTH ** -0.5),
        "w_out": normal(ks[14], (DEPTH, D_MODEL, D_MODEL), D_MODEL ** -0.5),
        "norm_ffn": gain(ks[15], (DEPTH, D_MODEL)),
        "w_router": normal(ks[16], (DEPTH, D_MODEL, N_EXPERTS), D_MODEL ** -0.5),
        "b_router": normal(ks[17], (DEPTH, N_EXPERTS), 0.01),
        "w_gate_up": normal(ks[18], (DEPTH, N_EXPERTS, D_MODEL, 2 * D_FF), D_MODEL ** -0.5),
        "b_gate_up": normal(ks[19], (DEPTH, N_EXPERTS, 2 * D_FF), 0.01),
        "w_down": normal(ks[20], (DEPTH, N_EXPERTS, D_FF, D_MODEL), D_FF ** -0.5),
        "b_down": normal(ks[21], (DEPTH, N_EXPERTS, D_MODEL), 0.01),
    }


def reference(x_prompt, x_sample, cache_kv_w128, cache_kv_w512, cache_kv_w2048, state_pool,
              norm_mix, w_in, q_norm, k_norm, w_pool_group, pool_scale, w_branch_attn, w_branch_pool,
              w_out, norm_ffn, w_router, b_router, w_gate_up, b_gate_up, w_down, b_down):
    caches = (cache_kv_w128, cache_kv_w512, cache_kv_w2048)
    xp, xs = x_prompt, x_sample
    kv_p = [[] for _ in ATTN_GROUPS]
    kv_s = [[] for _ in ATTN_GROUPS]
    pool_p, pool_s = [], []
    for layer in range(DEPTH):
        mix_w = (norm_mix[layer], w_in[layer], q_norm[layer], k_norm[layer], w_pool_group[layer],
                 pool_scale[layer], w_branch_attn[layer], w_branch_pool[layer], w_out[layer])
        ffn_w = (norm_ffn[layer], w_router[layer], b_router[layer], w_gate_up, b_gate_up, w_down, b_down)
        xp, kv_new_p, pool_new_p = token_mixer(xp, None, None, 0, *mix_w)
        xp = moe_ffn(xp, layer, *ffn_w)
        bufs = [c[layer] for c in caches]
        xs, kv_new_s, pool_new_s = token_mixer(xs, bufs, state_pool[layer], PAST_LEN, *mix_w)
        xs = moe_ffn(xs, layer, *ffn_w)
        for g in range(N_GROUPS):
            kv_p[g].append(kv_new_p[g])
            kv_s[g].append(kv_new_s[g])
        pool_p.append(pool_new_p)
        pool_s.append(pool_new_s)
    y_prompt, y_sample = xp, xs
    kv128_prompt, kv128_sample = jnp.stack(kv_p[0]), jnp.stack(kv_s[0])
    kv512_prompt, kv512_sample = jnp.stack(kv_p[1]), jnp.stack(kv_s[1])
    kv2048_prompt, kv2048_sample = jnp.stack(kv_p[2]), jnp.stack(kv_s[2])
    pool_prompt, pool_sample = jnp.stack(pool_p), jnp.stack(pool_s)
    return (y_prompt, y_sample, kv128_prompt, kv128_sample, kv512_prompt, kv512_sample,
            kv2048_prompt, kv2048_sample, pool_prompt, pool_sample)
```

```python
import functools

import jax
import jax.numpy as jnp
from jax import lax
from jax.experimental import pallas as pl
from jax.experimental.pallas import tpu as pltpu

F32 = jnp.float32
BF16 = jnp.bfloat16

HEAD_DIM = 128
HEADS = 8
ATTN_GROUPS = ((128, 1), (512, 4), (2048, 16))
BAND = 128
ATTN_ROWS = 2048
POOL_WINDOWS = (2, 4, 8, 16)
POOL_HALO = 16
TOP_K = 4
SWIGLU_LIMIT = 7.0
SWIGLU_ALPHA = 1.702
RMS_EPS = 1e-5
NEG_INF = -1e30
PAST_LEN = 2048

ROW_TILE = 640
COL_TILE = 512
TOK_BLOCK = 128
EXPERT_TILE = 256
VMEM_LIMIT = 56 * 1024 * 1024


def _params(semantics, vmem=VMEM_LIMIT):
    return pltpu.CompilerParams(dimension_semantics=semantics, vmem_limit_bytes=vmem)


def _rmsnorm_kernel(x_ref, g_ref, o_ref):
    x = x_ref[...]
    ms = jnp.mean(x * x, axis=-1, keepdims=True)
    o_ref[...] = (x * lax.rsqrt(ms + RMS_EPS) * g_ref[...]).astype(o_ref.dtype)


def _rmsnorm(x, gain):
    m, d = x.shape
    return pl.pallas_call(
        _rmsnorm_kernel,
        grid=(m // TOK_BLOCK,),
        in_specs=[pl.BlockSpec((TOK_BLOCK, d), lambda i: (i, 0)),
                  pl.BlockSpec((1, d), lambda i: (0, 0))],
        out_specs=pl.BlockSpec((TOK_BLOCK, d), lambda i: (i, 0)),
        out_shape=jax.ShapeDtypeStruct((m, d), BF16),
        compiler_params=_params(("parallel",)),
        name="rmsnorm",
    )(x, gain.reshape(1, d))


def _inproj_kernel(h_ref, w_ref, cg_ref, o_ref, wb_ref, *, qk_tiles, plain_end):
    j = pl.program_id(0)
    i = pl.program_id(1)

    @pl.when(i == 0)
    def _():
        wb_ref[...] = w_ref[...].astype(BF16)

    acc = jnp.dot(h_ref[...], wb_ref[...], preferred_element_type=F32)

    @pl.when(j < qk_tiles)
    def _():
        for hh in range(acc.shape[1] // HEAD_DIM):
            sl = slice(hh * HEAD_DIM, (hh + 1) * HEAD_DIM)
            blk = acc[:, sl]
            ms = jnp.mean(blk * blk, axis=-1, keepdims=True)
            o_ref[:, sl] = blk * lax.rsqrt(ms + RMS_EPS) * cg_ref[:, sl]

    @pl.when(jnp.logical_and(j >= qk_tiles, j < plain_end))
    def _():
        o_ref[...] = acc

    @pl.when(j >= plain_end)
    def _():
        o_ref[...] = jax.nn.sigmoid(acc)


def _inproj(h, w_in, colgain, qk_width, plain_width):
    m, d = h.shape
    n = w_in.shape[1]
    kern = functools.partial(_inproj_kernel, qk_tiles=qk_width // COL_TILE,
                             plain_end=(qk_width + plain_width) // COL_TILE)
    return pl.pallas_call(
        kern,
        grid=(n // COL_TILE, m // ROW_TILE),
        in_specs=[pl.BlockSpec((ROW_TILE, d), lambda j, i: (i, 0)),
                  pl.BlockSpec((d, COL_TILE), lambda j, i: (0, j)),
                  pl.BlockSpec((1, COL_TILE), lambda j, i: (0, j))],
        out_specs=pl.BlockSpec((ROW_TILE, COL_TILE), lambda j, i: (i, j)),
        out_shape=jax.ShapeDtypeStruct((m, n), F32),
        scratch_shapes=[pltpu.VMEM((d, COL_TILE), BF16)],
        compiler_params=_params(("arbitrary", "arbitrary")),
        name="inproj",
    )(h, w_in, colgain)


def _attn_prompt_kernel(q_ref, kh_ref, k_ref, vh_ref, v_ref, o_ref, lse_ref, *, dil):
    n = pl.program_id(1)
    span = BAND * dil
    qi = lax.broadcasted_iota(jnp.int32, (BAND, 2 * BAND), 0) + BAND
    ki = lax.broadcasted_iota(jnp.int32, (BAND, 2 * BAND), 1)
    dist = qi - ki
    band = (dist >= 0) & (dist <= BAND)
    first_key = jnp.where(n > 0, 0, BAND)
    band_first = band & (ki >= first_key)
    scale = HEAD_DIM ** -0.5

    def rows(ref, start, size):
        if dil == 1:
            return ref[pl.ds(start, size), :]
        return ref[pl.ds(start, size, stride=dil), :]

    for c in range(ATTN_ROWS // span):
        for r in range(dil):
            base = c * span + r
            q = rows(q_ref, base, BAND).astype(BF16)
            if c == 0:
                kc = jnp.concatenate([rows(kh_ref, r, BAND), rows(k_ref, r, BAND)], axis=0)
                vc = jnp.concatenate([rows(vh_ref, r, BAND), rows(v_ref, r, BAND)], axis=0)
                valid = band_first
            else:
                kc = rows(k_ref, base - span, 2 * BAND)
                vc = rows(v_ref, base - span, 2 * BAND)
                valid = band
            s = lax.dot_general(q, kc.astype(BF16), (((1,), (1,)), ((), ())),
                                preferred_element_type=F32) * scale
            s = jnp.where(valid, s, NEG_INF)
            m = jnp.max(s, axis=-1, keepdims=True)
            p = jnp.exp(s - m)
            l = jnp.sum(p, axis=-1, keepdims=True)
            o = jnp.dot(p.astype(BF16), vc.astype(BF16), preferred_element_type=F32) / l
            lse = jnp.broadcast_to(m + jnp.log(l), (BAND, HEAD_DIM))
            if dil == 1:
                o_ref[pl.ds(base, BAND), :] = o
                lse_ref[pl.ds(base, BAND), :] = lse
            else:
                o_ref[pl.ds(base, BAND, stride=dil), :] = o
                lse_ref[pl.ds(base, BAND, stride=dil), :] = lse


def _attn_prompt(z, g, dil, n_tok):
    hw = HEADS * HEAD_DIM
    n_grp = len(ATTN_GROUPS)
    span = BAND * dil
    spb = ATTN_ROWS // span
    qcol, kcol, vcol = (g * HEADS, (n_grp + g) * HEADS, (2 * n_grp + g) * HEADS)

    def cur(col):
        return pl.BlockSpec((ATTN_ROWS, HEAD_DIM), lambda h, n: (n, col + h))

    def halo(col):
        return pl.BlockSpec((span, HEAD_DIM),
                            lambda h, n: (jnp.maximum(n * spb - 1, 0), col + h))

    out_spec = pl.BlockSpec((ATTN_ROWS, HEAD_DIM), lambda h, n: (n, h))
    return pl.pallas_call(
        functools.partial(_attn_prompt_kernel, dil=dil),
        grid=(HEADS, n_tok // ATTN_ROWS),
        in_specs=[cur(qcol), halo(kcol), cur(kcol), halo(vcol), cur(vcol)],
        out_specs=[out_spec, out_spec],
        out_shape=[jax.ShapeDtypeStruct((n_tok, hw), F32)] * 2,
        compiler_params=_params(("parallel", "parallel")),
        name=f"attn_prompt_d{dil}",
    )(z, z, z, z, z)


def _attn_sample_kernel(q_ref, kn_ref, vn_ref, kv_ref, o_ref, lse_ref, *, rows):
    scale = HEAD_DIM ** -0.5

    def to_heads(row):
        return jnp.concatenate(
            [row[:, h * HEAD_DIM:(h + 1) * HEAD_DIM] for h in range(HEADS)], axis=0)

    def to_row(x):
        return jnp.concatenate([x[h:h + 1, :] for h in range(HEADS)], axis=1)

    o_rows, lse_rows = [], []
    for b in range(rows):
        q = to_heads(q_ref[b:b + 1, :])
        kn = to_heads(kn_ref[b:b + 1, :])
        vn = to_heads(vn_ref[b:b + 1, :])
        k = kv_ref[b, 0]
        v = kv_ref[b, 1]
        s = jnp.sum(k * q[None], axis=-1, keepdims=True) * scale
        sn = jnp.sum(q * kn, axis=-1, keepdims=True) * scale
        m = jnp.maximum(jnp.max(s, axis=0), sn)
        p = jnp.exp(s - m[None])
        pn = jnp.exp(sn - m)
        l = jnp.sum(p, axis=0) + pn
        o = (jnp.sum(p * v, axis=0) + pn * vn) / l
        lse = jnp.broadcast_to(m + jnp.log(l), (HEADS, HEAD_DIM))
        o_rows.append(to_row(o))
        lse_rows.append(to_row(lse))
    o_ref[...] = jnp.concatenate(o_rows, axis=0)
    lse_ref[...] = jnp.concatenate(lse_rows, axis=0)


def _attn_sample(z, cache, g, dil, row0, n_seq):
    hw = HEADS * HEAD_DIM
    n_grp = len(ATTN_GROUPS)
    length = cache.shape[2]
    assert length == BAND * dil, "cache must hold exactly one window"
    kv = cache.reshape(n_seq, 2, BAND, dil, HEADS, HEAD_DIM)
    rows = 8
    blk0 = row0 // rows

    def zspec(col):
        return pl.BlockSpec((rows, hw), lambda i: (blk0 + i, col))

    out_spec = pl.BlockSpec((rows, hw), lambda i: (i, 0))
    return pl.pallas_call(
        functools.partial(_attn_sample_kernel, rows=rows),
        grid=(n_seq // rows,),
        in_specs=[zspec(g), zspec(n_grp + g), zspec(2 * n_grp + g),
                  pl.BlockSpec((rows, 2, BAND, None, HEADS, HEAD_DIM),
                               lambda i: (i, 0, 0, 0, 0, 0))],
        out_specs=[out_spec, out_spec],
        out_shape=[jax.ShapeDtypeStruct((n_seq, hw), F32)] * 2,
        compiler_params=_params(("parallel",)),
        name=f"attn_sample_d{dil}",
    )(z, z, z, kv)


def _merge_kernel(*refs, n_prompt_blocks):
    n_grp = len(ATTN_GROUPS)
    po, pl_, so, sl_ = (refs[0:n_grp], refs[n_grp:2 * n_grp],
                        refs[2 * n_grp:3 * n_grp], refs[3 * n_grp:4 * n_grp])
    out_ref = refs[4 * n_grp]
    i = pl.program_id(0)

    def merge(o_refs, l_refs):
        ls = [r[...] for r in l_refs]
        mx = functools.reduce(jnp.maximum, ls)
        es = [jnp.exp(l - mx) for l in ls]
        den = functools.reduce(lambda a, b: a + b, es)
        acc = (es[0] / den) * o_refs[0][...]
        for g in range(1, n_grp):
            acc = acc + (es[g] / den) * o_refs[g][...]
        out_ref[...] = acc.astype(out_ref.dtype)

    @pl.when(i < n_prompt_blocks)
    def _():
        merge(po, pl_)

    @pl.when(i >= n_prompt_blocks)
    def _():
        merge(so, sl_)


def _merge_groups(prompt_o, prompt_lse, sample_o, sample_lse):
    n_tok = prompt_o[0].shape[0]
    n_seq = sample_o[0].shape[0]
    hw = HEADS * HEAD_DIM
    npb = n_tok // TOK_BLOCK
    nsb = n_seq // TOK_BLOCK

    pspec = pl.BlockSpec((TOK_BLOCK, hw), lambda i: (jnp.minimum(i, npb - 1), 0))
    sspec = pl.BlockSpec((TOK_BLOCK, hw), lambda i: (jnp.maximum(i - npb, 0), 0))
    n_grp = len(ATTN_GROUPS)
    return pl.pallas_call(
        functools.partial(_merge_kernel, n_prompt_blocks=npb),
        grid=(npb + nsb,),
        in_specs=[pspec] * (2 * n_grp) + [sspec] * (2 * n_grp),
        out_specs=pl.BlockSpec((TOK_BLOCK, hw), lambda i: (i, 0)),
        out_shape=jax.ShapeDtypeStruct((n_tok + n_seq, hw), BF16),
        compiler_params=_params(("parallel",)),
        name="merge_groups",
    )(*prompt_o, *prompt_lse, *sample_o, *sample_lse)


def _pool_kernel(*refs, rows, n_prompt_blocks, n_seq):
    ng = len(POOL_WINDOWS)
    cur_refs, halo_refs = refs[0:ng], refs[ng:2 * ng]
    st_ref, w_ref, s_ref, o_ref = refs[2 * ng:2 * ng + 4]
    i = pl.program_id(0)
    pg = cur_refs[0].shape[1]

    @pl.when(i < n_prompt_blocks)
    def _():
        pos = i * rows + lax.broadcasted_iota(jnp.int32, (rows, 1), 0)
        for g, win in enumerate(POOL_WINDOWS):
            cur = cur_refs[g][...]
            halo = jnp.where(i > 0, halo_refs[g][...], 0.0)
            a = jnp.concatenate([halo, cur], axis=0)
            sh = 1
            while sh < win:
                a = a + pltpu.roll(a, sh, 0)
                sh *= 2
            cnt = jnp.minimum(pos + 1, win).astype(F32)
            d = a[POOL_HALO:, :] / cnt - cur
            sl = slice(g * pg, (g + 1) * pg)
            y = jnp.dot(d.astype(BF16), w_ref[g].astype(BF16), preferred_element_type=F32)
            o_ref[:, sl] = (y * s_ref[:, sl]).astype(o_ref.dtype)

    @pl.when(i == n_prompt_blocks)
    def _():
        ctx = st_ref.shape[1]
        for g, win in enumerate(POOL_WINDOWS):
            sl = slice(g * pg, (g + 1) * pg)
            cur = cur_refs[g][0:n_seq, :]
            acc = cur
            for jj in range(1, win):
                acc = acc + st_ref[:, ctx - jj, sl]
            cnt = float(min(PAST_LEN + 1, win))
            d = acc / cnt - cur
            y = jnp.dot(d.astype(BF16), w_ref[g].astype(BF16), preferred_element_type=F32)
            o_ref[0:n_seq, sl] = (y * s_ref[:, sl]).astype(o_ref.dtype)


def _pool_branch(z, state, w_pool, pool_scale, n_tok, n_seq, u_col0):
    m = z.shape[0]
    ng, pg, _ = w_pool.shape
    pw = ng * pg
    rows = 256
    assert n_seq <= rows and m == n_tok + n_seq
    c0 = u_col0 // pg
    scale = pool_scale.reshape(1, pw)
    hpb = rows // POOL_HALO
    npb = n_tok // rows

    cur_specs = [pl.BlockSpec((rows, pg), lambda i, g=g: (i, c0 + g)) for g in range(ng)]
    halo_specs = [pl.BlockSpec((POOL_HALO, pg),
                               lambda i, g=g: (jnp.maximum(i * hpb - 1, 0), c0 + g))
                  for g in range(ng)]
    return pl.pallas_call(
        functools.partial(_pool_kernel, rows=rows, n_prompt_blocks=npb, n_seq=n_seq),
        grid=(npb + 1,),
        in_specs=cur_specs + halo_specs + [
            pl.BlockSpec(state.shape, lambda i: (0, 0, 0)),
            pl.BlockSpec((ng, pg, pg), lambda i: (0, 0, 0)),
            pl.BlockSpec((1, pw), lambda i: (0, 0))],
        out_specs=pl.BlockSpec((rows, pw), lambda i: (i, 0)),
        out_shape=jax.ShapeDtypeStruct((m, pw), BF16),
        compiler_params=_params(("parallel",)),
        name="pool",
    )(*([z] * (2 * ng)), state, w_pool, scale)


def _mix_kernel(oa_ref, po_ref, wa_ref, wp_ref, ga_ref, gb_ref, o_ref, wab_ref, wpb_ref):
    @pl.when(pl.program_id(1) == 0)
    def _():
        wab_ref[...] = wa_ref[...].astype(BF16)
        wpb_ref[...] = wp_ref[...].astype(BF16)

    a = jnp.dot(oa_ref[...], wab_ref[...], preferred_element_type=F32)
    p = jnp.dot(po_ref[...], wpb_ref[...], preferred_element_type=F32)
    o_ref[...] = (ga_ref[...] * a + gb_ref[...] * p).astype(o_ref.dtype)


def _mix(o_attn, pool_out, w_a, w_p, z, gate_col0):
    m, ka = o_attn.shape
    kp = pool_out.shape[1]
    n = w_a.shape[1]
    ga0 = gate_col0 // COL_TILE
    gb0 = (gate_col0 + n) // COL_TILE
    return pl.pallas_call(
        _mix_kernel,
        grid=(n // COL_TILE, m // ROW_TILE),
        in_specs=[pl.BlockSpec((ROW_TILE, ka), lambda j, i: (i, 0)),
                  pl.BlockSpec((ROW_TILE, kp), lambda j, i: (i, 0)),
                  pl.BlockSpec((ka, COL_TILE), lambda j, i: (0, j)),
                  pl.BlockSpec((kp, COL_TILE), lambda j, i: (0, j)),
                  pl.BlockSpec((ROW_TILE, COL_TILE), lambda j, i: (i, ga0 + j)),
                  pl.BlockSpec((ROW_TILE, COL_TILE), lambda j, i: (i, gb0 + j))],
        out_specs=pl.BlockSpec((ROW_TILE, COL_TILE), lambda j, i: (i, j)),
        out_shape=jax.ShapeDtypeStruct((m, n), BF16),
        scratch_shapes=[pltpu.VMEM((ka, COL_TILE), BF16), pltpu.VMEM((kp, COL_TILE), BF16)],
        compiler_params=_params(("arbitrary", "arbitrary")),
        name="mix",
    )(o_attn, pool_out, w_a, w_p, z, z)


def _outproj_kernel(a_ref, w_ref, x_ref, o_ref, wb_ref):
    @pl.when(pl.program_id(1) == 0)
    def _():
        wb_ref[...] = w_ref[...].astype(BF16)

    o_ref[...] = x_ref[...] + jnp.dot(a_ref[...], wb_ref[...], preferred_element_type=F32)


def _outproj(mixed, w_out, x):
    m, k = mixed.shape
    n = w_out.shape[1]
    return pl.pallas_call(
        _outproj_kernel,
        grid=(n // COL_TILE, m // ROW_TILE),
        in_specs=[pl.BlockSpec((ROW_TILE, k), lambda j, i: (i, 0)),
                  pl.BlockSpec((k, COL_TILE), lambda j, i: (0, j)),
                  pl.BlockSpec((ROW_TILE, COL_TILE), lambda j, i: (i, j))],
        out_specs=pl.BlockSpec((ROW_TILE, COL_TILE), lambda j, i: (i, j)),
        out_shape=jax.ShapeDtypeStruct((m, n), F32),
        scratch_shapes=[pltpu.VMEM((k, COL_TILE), BF16)],
        compiler_params=_params(("arbitrary", "arbitrary")),
        name="outproj",
    )(mixed, w_out, x)


def _router_kernel(x_ref, g_ref, wr_ref, br_ref, h_ref, e_ref, r_ref, w_ref, c_ref, carry_ref):
    i = pl.program_id(0)

    @pl.when(i == 0)
    def _():
        carry_ref[...] = jnp.zeros_like(carry_ref)

    x = x_ref[...]
    ms = jnp.mean(x * x, axis=-1, keepdims=True)
    h = x * lax.rsqrt(ms + RMS_EPS) * g_ref[...]
    h_ref[...] = h
    logits = jnp.dot(h.astype(BF16), wr_ref[...].astype(BF16),
                     preferred_element_type=F32) + br_ref[...]
    tb, ne = logits.shape
    elane = lax.broadcasted_iota(jnp.int32, (tb, ne), 1).astype(F32)
    work = logits
    sel = jnp.zeros((tb, ne), F32)
    idxs, vals = [], []
    for _ in range(TOP_K):
        mx = jnp.max(work, axis=-1, keepdims=True)
        idx = jnp.min(jnp.where(work == mx, elane, float(ne)), axis=-1, keepdims=True)
        hit = elane == idx
        sel = jnp.where(hit, 1.0, sel)
        work = jnp.where(hit, -jnp.inf, work)
        idxs.append(idx)
        vals.append(mx)
    exps = [jnp.exp(v - vals[0]) for v in vals]
    den = functools.reduce(lambda a, b: a + b, exps)

    ri = lax.broadcasted_iota(jnp.int32, (tb, tb), 0)
    ci = lax.broadcasted_iota(jnp.int32, (tb, tb), 1)
    tril = jnp.where(ci < ri, 1.0, 0.0).astype(BF16)
    before = jnp.dot(tril, sel.astype(BF16), preferred_element_type=F32) + carry_ref[0:1, :]
    carry_ref[...] = carry_ref[...] + jnp.sum(sel, axis=0, keepdims=True)

    lane = lax.broadcasted_iota(jnp.int32, (tb, HEAD_DIM), 1)
    e_out = jnp.zeros((tb, HEAD_DIM), jnp.int32)
    r_out = jnp.zeros((tb, HEAD_DIM), jnp.int32)
    w_out = jnp.zeros((tb, HEAD_DIM), F32)
    for k in range(TOP_K):
        rank = jnp.sum(jnp.where(elane == idxs[k], before, 0.0), axis=-1, keepdims=True)
        e_out = jnp.where(lane == k, idxs[k].astype(jnp.int32), e_out)
        r_out = jnp.where(lane == k, rank.astype(jnp.int32), r_out)
        w_out = jnp.where(lane == k, exps[k] / den, w_out)
    e_ref[...] = e_out
    r_ref[...] = r_out
    w_ref[...] = w_out
    c_ref[...] = carry_ref[...].astype(jnp.int32)


def _router(x1, gain, w_router, b_router):
    m, d = x1.shape
    ne = w_router.shape[1]
    nblk = m // TOK_BLOCK
    lanes = HEAD_DIM
    return pl.pallas_call(
        _router_kernel,
        grid=(nblk,),
        in_specs=[pl.BlockSpec((TOK_BLOCK, d), lambda i: (i, 0)),
                  pl.BlockSpec((1, d), lambda i: (0, 0)),
                  pl.BlockSpec((d, ne), lambda i: (0, 0)),
                  pl.BlockSpec((1, ne), lambda i: (0, 0))],
        out_specs=[pl.BlockSpec((TOK_BLOCK, d), lambda i: (i, 0)),
                   pl.BlockSpec((TOK_BLOCK, lanes), lambda i: (i, 0)),
                   pl.BlockSpec((TOK_BLOCK, lanes), lambda i: (i, 0)),
                   pl.BlockSpec((TOK_BLOCK, lanes), lambda i: (i, 0)),
                   pl.BlockSpec((8, ne), lambda i: (0, 0))],
        out_shape=[jax.ShapeDtypeStruct((m, d), F32),
                   jax.ShapeDtypeStruct((m, lanes), jnp.int32),
                   jax.ShapeDtypeStruct((m, lanes), jnp.int32),
                   jax.ShapeDtypeStruct((m, lanes), F32),
                   jax.ShapeDtypeStruct((8, ne), jnp.int32)],
        scratch_shapes=[pltpu.VMEM((8, ne), F32)],
        compiler_params=_params(("arbitrary",)),
        name="router",
    )(x1, gain.reshape(1, d), w_router, b_router.reshape(1, ne))


def _dispatch_kernel(tok_ref, h_hbm, o_ref, buf_ref, sem):
    rows = buf_ref.shape[0]

    def row_copy(r):
        return pltpu.make_async_copy(h_hbm.at[pl.ds(tok_ref[0, 0, r], 1), :],
                                     buf_ref.at[pl.ds(r, 1), :], sem)

    def start(r, c):
        row_copy(r).start()
        return c

    def wait(r, c):
        row_copy(r).wait()
        return c

    lax.fori_loop(0, rows, start, 0)
    lax.fori_loop(0, rows, wait, 0)
    o_ref[...] = buf_ref[...].astype(o_ref.dtype)


def _dispatch(h, row_tok, n_rows):
    d = h.shape[1]
    nu = n_rows // EXPERT_TILE
    return pl.pallas_call(
        _dispatch_kernel,
        grid=(nu,),
        in_specs=[pl.BlockSpec((1, 1, EXPERT_TILE), lambda u: (u, 0, 0),
                               memory_space=pltpu.SMEM),
                  pl.BlockSpec(memory_space=pl.ANY)],
        out_specs=pl.BlockSpec((EXPERT_TILE, d), lambda u: (u, 0)),
        out_shape=jax.ShapeDtypeStruct((n_rows, d), BF16),
        scratch_shapes=[pltpu.VMEM((EXPERT_TILE, d), F32), pltpu.SemaphoreType.DMA(())],
        compiler_params=_params(("arbitrary",)),
        name="dispatch",
    )(row_tok.reshape(nu, 1, EXPERT_TILE), h)


def _gate_up_kernel(ue_ref, ub_ref, uf_ref, na_ref, x_ref, wg_ref, wu_ref, bg_ref, bu_ref,
                    o_ref, wgb_ref, wub_ref):
    del ue_ref, ub_ref
    u = pl.program_id(1)

    @pl.when(uf_ref[u] == 1)
    def _():
        wgb_ref[...] = wg_ref[...].astype(BF16)
        wub_ref[...] = wu_ref[...].astype(BF16)

    @pl.when(u < na_ref[0])
    def _():
        x = x_ref[...]
        gate = jnp.dot(x, wgb_ref[...], preferred_element_type=F32) + bg_ref[...]
        up = jnp.dot(x, wub_ref[...], preferred_element_type=F32) + bu_ref[...]
        gate = jnp.minimum(gate, SWIGLU_LIMIT)
        up = jnp.clip(up, -SWIGLU_LIMIT, SWIGLU_LIMIT)
        act = gate * jax.nn.sigmoid(SWIGLU_ALPHA * gate) * (up + 1.0)
        o_ref[...] = act.astype(o_ref.dtype)

    @pl.when(u >= na_ref[0])
    def _():
        o_ref[...] = jnp.zeros_like(o_ref)


def _gate_up(xs, w_gate_up, b_gate_up, unit_e, unit_blk, unit_first, n_active):
    n_rows, d = xs.shape
    ne = w_gate_up.shape[1]
    ff = w_gate_up.shape[3] // 2
    nu = n_rows // EXPERT_TILE
    nf = ff // COL_TILE
    bias = b_gate_up.reshape(ne, 1, 2 * ff)
    grid_spec = pltpu.PrefetchScalarGridSpec(
        num_scalar_prefetch=4,
        grid=(nf, nu),
        in_specs=[
            pl.BlockSpec((EXPERT_TILE, d), lambda f, u, ue, ub, uf, na: (ub[u], 0)),
            pl.BlockSpec((None, None, d, COL_TILE), lambda f, u, ue, ub, uf, na: (0, ue[u], 0, f)),
            pl.BlockSpec((None, None, d, COL_TILE),
                         lambda f, u, ue, ub, uf, na: (0, ue[u], 0, nf + f)),
            pl.BlockSpec((None, 1, COL_TILE), lambda f, u, ue, ub, uf, na: (ue[u], 0, f)),
            pl.BlockSpec((None, 1, COL_TILE), lambda f, u, ue, ub, uf, na: (ue[u], 0, nf + f)),
        ],
        out_specs=pl.BlockSpec((EXPERT_TILE, COL_TILE), lambda f, u, ue, ub, uf, na: (u, f)),
        scratch_shapes=[pltpu.VMEM((d, COL_TILE), BF16), pltpu.VMEM((d, COL_TILE), BF16)],
    )
    return pl.pallas_call(
        _gate_up_kernel,
        grid_spec=grid_spec,
        out_shape=jax.ShapeDtypeStruct((n_rows, ff), BF16),
        compiler_params=_params(("arbitrary", "arbitrary")),
        name="expert_gate_up",
    )(unit_e, unit_blk, unit_first, n_active, xs, w_gate_up, w_gate_up, bias, bias)


def _down_kernel(ue_ref, ub_ref, uf_ref, na_ref, a_ref, w_ref, b_ref, o_ref, wb_ref):
    del ue_ref, ub_ref
    u = pl.program_id(1)

    @pl.when(uf_ref[u] == 1)
    def _():
        wb_ref[...] = w_ref[...].astype(BF16)

    @pl.when(u < na_ref[0])
    def _():
        o_ref[...] = jnp.dot(a_ref[...], wb_ref[...], preferred_element_type=F32) + b_ref[...]

    @pl.when(u >= na_ref[0])
    def _():
        o_ref[...] = jnp.zeros_like(o_ref)


def _down(act, w_down, b_down, unit_e, unit_blk, unit_first, n_active):
    n_rows, ff = act.shape
    ne = w_down.shape[1]
    d = w_down.shape[3]
    nu = n_rows // EXPERT_TILE
    bias = b_down.reshape(ne, 1, d)
    grid_spec = pltpu.PrefetchScalarGridSpec(
        num_scalar_prefetch=4,
        grid=(d // COL_TILE, nu),
        in_specs=[
            pl.BlockSpec((EXPERT_TILE, ff), lambda n, u, ue, ub, uf, na: (ub[u], 0)),
            pl.BlockSpec((None, None, ff, COL_TILE), lambda n, u, ue, ub, uf, na: (0, ue[u], 0, n)),
            pl.BlockSpec((None, 1, COL_TILE), lambda n, u, ue, ub, uf, na: (ue[u], 0, n)),
        ],
        out_specs=pl.BlockSpec((EXPERT_TILE, COL_TILE), lambda n, u, ue, ub, uf, na: (u, n)),
        scratch_shapes=[pltpu.VMEM((ff, COL_TILE), BF16)],
    )
    return pl.pallas_call(
        _down_kernel,
        grid_spec=grid_spec,
        out_shape=jax.ShapeDtypeStruct((n_rows, d), F32),
        compiler_params=_params(("arbitrary", "arbitrary")),
        name="expert_down",
    )(unit_e, unit_blk, unit_first, n_active, act, w_down, bias)


def _combine_kernel(pos_ref, x_ref, w_ref, y_hbm, o_ref, buf_ref, sem):
    tb = x_ref.shape[0]

    def row_copy(t, k):
        return pltpu.make_async_copy(y_hbm.at[pl.ds(pos_ref[0, 0, t * TOP_K + k], 1), :],
                                     buf_ref.at[k, pl.ds(t, 1), :], sem)

    def start(t, c):
        for k in range(TOP_K):
            row_copy(t, k).start()
        return c

    def wait(t, c):
        for k in range(TOP_K):
            row_copy(t, k).wait()
        return c

    lax.fori_loop(0, tb, start, 0)
    lax.fori_loop(0, tb, wait, 0)
    w = w_ref[...]
    acc = w[:, 0:1] * buf_ref[0]
    for k in range(1, TOP_K):
        acc = acc + w[:, k:k + 1] * buf_ref[k]
    o_ref[...] = x_ref[...] + acc


def _combine(x1, wts, pos, y_sorted):
    m, d = x1.shape
    nblk = m // TOK_BLOCK
    return pl.pallas_call(
        _combine_kernel,
        grid=(nblk,),
        in_specs=[pl.BlockSpec((1, 1, TOK_BLOCK * TOP_K), lambda i: (i, 0, 0),
                               memory_space=pltpu.SMEM),
                  pl.BlockSpec((TOK_BLOCK, d), lambda i: (i, 0)),
                  pl.BlockSpec((TOK_BLOCK, wts.shape[1]), lambda i: (i, 0)),
                  pl.BlockSpec(memory_space=pl.ANY)],
        out_specs=pl.BlockSpec((TOK_BLOCK, d), lambda i: (i, 0)),
        out_shape=jax.ShapeDtypeStruct((m, d), F32),
        scratch_shapes=[pltpu.VMEM((TOP_K, TOK_BLOCK, d), F32), pltpu.SemaphoreType.DMA(())],
        compiler_params=_params(("arbitrary",)),
        name="combine",
    )(pos.reshape(nblk, 1, TOK_BLOCK * TOP_K), x1, wts, y_sorted)


def _moe(x1, layer_w):
    norm_ffn, w_router, b_router, w_gate_up, b_gate_up, w_down, b_down = layer_w
    m = x1.shape[0]
    ne = w_router.shape[1]
    h, eidx, rank, wts, counts = _router(x1, norm_ffn, w_router, b_router)

    tile = EXPERT_TILE
    counts = counts[0]
    padded = (counts + tile - 1) // tile * tile
    pend = jnp.cumsum(padded)
    pstart = pend - padded
    eidx = eidx[:, :TOP_K]
    pos = (pstart[eidx] + rank[:, :TOP_K]).astype(jnp.int32)
    n_rows = (m * TOP_K + ne * (tile - 1) + tile - 1) // tile * tile
    nu = n_rows // tile
    tok = jnp.broadcast_to(jnp.arange(m, dtype=jnp.int32)[:, None], pos.shape)
    row_tok = jnp.zeros((n_rows,), jnp.int32).at[pos.reshape(-1)].set(tok.reshape(-1))
    n_active = (pend[-1] // tile).astype(jnp.int32)
    u = jnp.arange(nu, dtype=jnp.int32)
    unit_blk = jnp.minimum(u, n_active - 1)
    unit_e = jnp.minimum(jnp.searchsorted(pend, unit_blk * tile, side="right"), ne - 1)
    unit_e = unit_e.astype(jnp.int32)
    prev_e = jnp.concatenate([jnp.full((1,), -1, jnp.int32), unit_e[:-1]])
    unit_first = ((u < n_active) & (unit_e != prev_e)).astype(jnp.int32)
    n_act = n_active.reshape(1)

    xs = _dispatch(h, row_tok, n_rows)
    act = _gate_up(xs, w_gate_up, b_gate_up, unit_e, unit_blk, unit_first, n_act)
    ys = _down(act, w_down, b_down, unit_e, unit_blk, unit_first, n_act)
    return _combine(x1, wts, pos, ys)


def kernel(x_prompt, x_sample, cache_kv_w128, cache_kv_w512, cache_kv_w2048, state_pool,
           norm_mix, w_in, q_norm, k_norm, w_pool_group, pool_scale, w_branch_attn,
           w_branch_pool, w_out, norm_ffn, w_router, b_router, w_gate_up, b_gate_up,
           w_down, b_down):
    depth = norm_mix.shape[0]
    assert depth == 1 and x_prompt.shape[0] == 1 and x_sample.shape[1] == 1
    n_tok, d_model = x_prompt.shape[1], x_prompt.shape[2]
    n_seq = x_sample.shape[0]
    caches = (cache_kv_w128, cache_kv_w512, cache_kv_w2048)
    n_grp = len(ATTN_GROUPS)
    hw = HEADS * HEAD_DIM
    qkv_w = n_grp * hw
    pool_w = w_pool_group.shape[1] * w_pool_group.shape[2]
    width = w_in.shape[2]
    layer = 0

    x = jnp.concatenate([x_prompt[0], x_sample[:, 0]], axis=0)

    ones = jnp.ones((width - 2 * qkv_w,), F32)
    colgain = jnp.concatenate([
        jnp.broadcast_to(q_norm[layer][:, None, :], (n_grp, HEADS, HEAD_DIM)).reshape(-1),
        jnp.broadcast_to(k_norm[layer][:, None, :], (n_grp, HEADS, HEAD_DIM)).reshape(-1),
        ones]).reshape(1, width)

    h = _rmsnorm(x, norm_mix[layer])
    z = _inproj(h, w_in[layer], colgain, 2 * qkv_w, qkv_w + pool_w)

    po, plse, so, slse = [], [], [], []
    for g, (window, dil) in enumerate(ATTN_GROUPS):
        assert window == BAND * dil
        o, lse = _attn_prompt(z, g, dil, n_tok)
        po.append(o)
        plse.append(lse)
        o, lse = _attn_sample(z, caches[g][layer], g, dil, n_tok, n_seq)
        so.append(o)
        slse.append(lse)
    o_attn = _merge_groups(po, plse, so, slse)

    pool_out = _pool_branch(z, state_pool[layer], w_pool_group[layer], pool_scale[layer],
                            n_tok, n_seq, 3 * qkv_w)
    mixed = _mix(o_attn, pool_out, w_branch_attn[layer], w_branch_pool[layer], z,
                 3 * qkv_w + pool_w)
    x1 = _outproj(mixed, w_out[layer], x)
    y = _moe(x1, (norm_ffn[layer], w_router[layer], b_router[layer],
                  w_gate_up, b_gate_up, w_down, b_down))

    y_prompt = y[:n_tok][None]
    y_sample = y[n_tok:][:, None]

    outs = [y_prompt, y_sample]
    k_all = z[:, qkv_w:2 * qkv_w]
    v_all = z[:, 2 * qkv_w:3 * qkv_w]
    for g, (window, dil) in enumerate(ATTN_GROUPS):
        keep = min(window, n_tok)
        sl = slice(g * hw, (g + 1) * hw)
        kp = k_all[n_tok - keep:n_tok, sl].reshape(keep, HEADS, HEAD_DIM)
        vp = v_all[n_tok - keep:n_tok, sl].reshape(keep, HEADS, HEAD_DIM)
        outs.append(jnp.stack([kp, vp], axis=0)[None, None])
        ks = k_all[n_tok:, sl].reshape(n_seq, 1, HEADS, HEAD_DIM)
        vs = v_all[n_tok:, sl].reshape(n_seq, 1, HEADS, HEAD_DIM)
        outs.append(jnp.stack([ks, vs], axis=1)[None])
    u_all = z[:, 3 * qkv_w:3 * qkv_w + pool_w]
    keep = min(POOL_HALO - 1, n_tok)
    outs.append(u_all[n_tok - keep:n_tok][None, None])
    outs.append(u_all[n_tok:][:, None][None])
    return tuple(outs)
```

```python
import functools

import jax
import jax.numpy as jnp
from jax import lax
from jax.experimental import pallas as pl
from jax.experimental.pallas import tpu as pltpu

F32 = jnp.float32
BF16 = jnp.bfloat16

HEAD_DIM = 128
HEADS = 8
ATTN_GROUPS = ((128, 1), (512, 4), (2048, 16))
BAND = 128
ATTN_ROWS = 2048
POOL_WINDOWS = (2, 4, 8, 16)
POOL_HALO = 16
TOP_K = 4
SWIGLU_LIMIT = 7.0
SWIGLU_ALPHA = 1.702
RMS_EPS = 1e-5
NEG_INF = -1e30
PAST_LEN = 2048

ROW_TILE = 640
COL_TILE = 512
TOK_BLOCK = 128
EXPERT_TILE = 256
VMEM_LIMIT = 56 * 1024 * 1024


def _params(semantics, vmem=VMEM_LIMIT):
    return pltpu.CompilerParams(dimension_semantics=semantics, vmem_limit_bytes=vmem)


def _rmsnorm_kernel(xp_ref, xs_ref, g_ref, o_ref, *, n_prompt_blocks):
    i = pl.program_id(0)

    def norm(x_ref):
        x = x_ref[...]
        ms = jnp.mean(x * x, axis=-1, keepdims=True)
        o_ref[...] = (x * lax.rsqrt(ms + RMS_EPS) * g_ref[...]).astype(o_ref.dtype)

    @pl.when(i < n_prompt_blocks)
    def _():
        norm(xp_ref)

    @pl.when(i >= n_prompt_blocks)
    def _():
        norm(xs_ref)


def _rmsnorm(xp, xs, gain):
    n_tok, d = xp.shape
    n_seq = xs.shape[0]
    npb, nsb = n_tok // TOK_BLOCK, n_seq // TOK_BLOCK
    return pl.pallas_call(
        functools.partial(_rmsnorm_kernel, n_prompt_blocks=npb),
        grid=(npb + nsb,),
        in_specs=[pl.BlockSpec((TOK_BLOCK, d), lambda i: (jnp.minimum(i, npb - 1), 0)),
                  pl.BlockSpec((TOK_BLOCK, d), lambda i: (jnp.maximum(i - npb, 0), 0)),
                  pl.BlockSpec((1, d), lambda i: (0, 0))],
        out_specs=pl.BlockSpec((TOK_BLOCK, d), lambda i: (i, 0)),
        out_shape=jax.ShapeDtypeStruct((n_tok + n_seq, d), BF16),
        compiler_params=_params(("parallel",)),
        name="rmsnorm",
    )(xp, xs, gain.reshape(1, d))


def _inproj_kernel(h_ref, w_ref, cg_ref, o_ref, wb_ref, *, qk_tiles, plain_end):
    j = pl.program_id(0)
    i = pl.program_id(1)

    @pl.when(i == 0)
    def _():
        wb_ref[...] = w_ref[...].astype(BF16)

    acc = jnp.dot(h_ref[...], wb_ref[...], preferred_element_type=F32)

    @pl.when(j < qk_tiles)
    def _():
        for hh in range(acc.shape[1] // HEAD_DIM):
            sl = slice(hh * HEAD_DIM, (hh + 1) * HEAD_DIM)
            blk = acc[:, sl]
            ms = jnp.mean(blk * blk, axis=-1, keepdims=True)
            o_ref[:, sl] = blk * lax.rsqrt(ms + RMS_EPS) * cg_ref[:, sl]

    @pl.when(jnp.logical_and(j >= qk_tiles, j < plain_end))
    def _():
        o_ref[...] = acc

    @pl.when(j >= plain_end)
    def _():
        o_ref[...] = jax.nn.sigmoid(acc)


def _inproj(h, w_in, colgain, qk_width, plain_width):
    m, d = h.shape
    n = w_in.shape[1]
    kern = functools.partial(_inproj_kernel, qk_tiles=qk_width // COL_TILE,
                             plain_end=(qk_width + plain_width) // COL_TILE)
    return pl.pallas_call(
        kern,
        grid=(n // COL_TILE, m // ROW_TILE),
        in_specs=[pl.BlockSpec((ROW_TILE, d), lambda j, i: (i, 0)),
                  pl.BlockSpec((d, COL_TILE), lambda j, i: (0, j)),
                  pl.BlockSpec((1, COL_TILE), lambda j, i: (0, j))],
        out_specs=pl.BlockSpec((ROW_TILE, COL_TILE), lambda j, i: (i, j)),
        out_shape=jax.ShapeDtypeStruct((m, n), F32),
        scratch_shapes=[pltpu.VMEM((d, COL_TILE), BF16)],
        compiler_params=_params(("arbitrary", "arbitrary")),
        name="inproj",
    )(h, w_in, colgain)


def _attn_prompt_kernel(q_ref, kh_ref, k_ref, vh_ref, v_ref, o_ref, lse_ref, *, dil):
    n = pl.program_id(1)
    span = BAND * dil
    qi = lax.broadcasted_iota(jnp.int32, (BAND, 2 * BAND), 0) + BAND
    ki = lax.broadcasted_iota(jnp.int32, (BAND, 2 * BAND), 1)
    dist = qi - ki
    band = (dist >= 0) & (dist <= BAND)
    first_key = jnp.where(n > 0, 0, BAND)
    band_first = band & (ki >= first_key)
    scale = HEAD_DIM ** -0.5

    def rows(ref, start, size):
        if dil == 1:
            return ref[pl.ds(start, size), :]
        return ref[pl.ds(start, size, stride=dil), :]

    for c in range(ATTN_ROWS // span):
        for r in range(dil):
            base = c * span + r
            q = rows(q_ref, base, BAND).astype(BF16)
            if c == 0:
                kc = jnp.concatenate([rows(kh_ref, r, BAND), rows(k_ref, r, BAND)], axis=0)
                vc = jnp.concatenate([rows(vh_ref, r, BAND), rows(v_ref, r, BAND)], axis=0)
                valid = band_first
            else:
                kc = rows(k_ref, base - span, 2 * BAND)
                vc = rows(v_ref, base - span, 2 * BAND)
                valid = band
            s = lax.dot_general(q, kc.astype(BF16), (((1,), (1,)), ((), ())),
                                preferred_element_type=F32) * scale
            s = jnp.where(valid, s, NEG_INF)
            m = jnp.max(s, axis=-1, keepdims=True)
            p = jnp.exp(s - m)
            l = jnp.sum(p, axis=-1, keepdims=True)
            o = jnp.dot(p.astype(BF16), vc.astype(BF16), preferred_element_type=F32) / l
            lse = jnp.broadcast_to(m + jnp.log(l), (BAND, HEAD_DIM))
            if dil == 1:
                o_ref[pl.ds(base, BAND), :] = o
                lse_ref[pl.ds(base, BAND), :] = lse
            else:
                o_ref[pl.ds(base, BAND, stride=dil), :] = o
                lse_ref[pl.ds(base, BAND, stride=dil), :] = lse


def _attn_prompt(z, g, dil, n_tok):
    hw = HEADS * HEAD_DIM
    n_grp = len(ATTN_GROUPS)
    span = BAND * dil
    spb = ATTN_ROWS // span
    qcol, kcol, vcol = (g * HEADS, (n_grp + g) * HEADS, (2 * n_grp + g) * HEADS)

    def cur(col):
        return pl.BlockSpec((ATTN_ROWS, HEAD_DIM), lambda h, n: (n, col + h))

    def halo(col):
        return pl.BlockSpec((span, HEAD_DIM),
                            lambda h, n: (jnp.maximum(n * spb - 1, 0), col + h))

    out_spec = pl.BlockSpec((ATTN_ROWS, HEAD_DIM), lambda h, n: (n, h))
    return pl.pallas_call(
        functools.partial(_attn_prompt_kernel, dil=dil),
        grid=(HEADS, n_tok // ATTN_ROWS),
        in_specs=[cur(qcol), halo(kcol), cur(kcol), halo(vcol), cur(vcol)],
        out_specs=[out_spec, out_spec],
        out_shape=[jax.ShapeDtypeStruct((n_tok, hw), F32)] * 2,
        compiler_params=_params(("parallel", "parallel")),
        name=f"attn_prompt_d{dil}",
    )(z, z, z, z, z)


def _attn_sample_kernel(q_ref, kn_ref, vn_ref, kv_ref, o_ref, lse_ref, *, rows):
    scale = HEAD_DIM ** -0.5

    def to_heads(row):
        return jnp.concatenate(
            [row[:, h * HEAD_DIM:(h + 1) * HEAD_DIM] for h in range(HEADS)], axis=0)

    def to_row(x):
        return jnp.concatenate([x[h:h + 1, :] for h in range(HEADS)], axis=1)

    o_rows, lse_rows = [], []
    for b in range(rows):
        q = to_heads(q_ref[b:b + 1, :])
        kn = to_heads(kn_ref[b:b + 1, :])
        vn = to_heads(vn_ref[b:b + 1, :])
        k = kv_ref[b, 0]
        v = kv_ref[b, 1]
        s = jnp.sum(k * q[None], axis=-1, keepdims=True) * scale
        sn = jnp.sum(q * kn, axis=-1, keepdims=True) * scale
        m = jnp.maximum(jnp.max(s, axis=0), sn)
        p = jnp.exp(s - m[None])
        pn = jnp.exp(sn - m)
        l = jnp.sum(p, axis=0) + pn
        o = (jnp.sum(p * v, axis=0) + pn * vn) / l
        lse = jnp.broadcast_to(m + jnp.log(l), (HEADS, HEAD_DIM))
        o_rows.append(to_row(o))
        lse_rows.append(to_row(lse))
    o_ref[...] = jnp.concatenate(o_rows, axis=0)
    lse_ref[...] = jnp.concatenate(lse_rows, axis=0)


def _attn_sample(z, cache, g, dil, row0, n_seq):
    hw = HEADS * HEAD_DIM
    n_grp = len(ATTN_GROUPS)
    length = cache.shape[2]
    assert length == BAND * dil, "cache must hold exactly one window"
    kv = cache.reshape(n_seq, 2, BAND, dil, HEADS, HEAD_DIM)
    rows = 8
    blk0 = row0 // rows

    def zspec(col):
        return pl.BlockSpec((rows, hw), lambda i: (blk0 + i, col))

    out_spec = pl.BlockSpec((rows, hw), lambda i: (i, 0))
    return pl.pallas_call(
        functools.partial(_attn_sample_kernel, rows=rows),
        grid=(n_seq // rows,),
        in_specs=[zspec(g), zspec(n_grp + g), zspec(2 * n_grp + g),
                  pl.BlockSpec((rows, 2, BAND, None, HEADS, HEAD_DIM),
                               lambda i: (i, 0, 0, 0, 0, 0))],
        out_specs=[out_spec, out_spec],
        out_shape=[jax.ShapeDtypeStruct((n_seq, hw), F32)] * 2,
        compiler_params=_params(("parallel",)),
        name=f"attn_sample_d{dil}",
    )(z, z, z, kv)


def _merge_kernel(*refs, n_prompt_blocks):
    n_grp = len(ATTN_GROUPS)
    po, pl_, so, sl_ = (refs[0:n_grp], refs[n_grp:2 * n_grp],
                        refs[2 * n_grp:3 * n_grp], refs[3 * n_grp:4 * n_grp])
    out_ref = refs[4 * n_grp]
    i = pl.program_id(0)

    def merge(o_refs, l_refs):
        ls = [r[...] for r in l_refs]
        mx = functools.reduce(jnp.maximum, ls)
        es = [jnp.exp(l - mx) for l in ls]
        den = functools.reduce(lambda a, b: a + b, es)
        acc = (es[0] / den) * o_refs[0][...]
        for g in range(1, n_grp):
            acc = acc + (es[g] / den) * o_refs[g][...]
        out_ref[...] = acc.astype(out_ref.dtype)

    @pl.when(i < n_prompt_blocks)
    def _():
        merge(po, pl_)

    @pl.when(i >= n_prompt_blocks)
    def _():
        merge(so, sl_)


def _merge_groups(prompt_o, prompt_lse, sample_o, sample_lse):
    n_tok = prompt_o[0].shape[0]
    n_seq = sample_o[0].shape[0]
    hw = HEADS * HEAD_DIM
    npb = n_tok // TOK_BLOCK
    nsb = n_seq // TOK_BLOCK

    pspec = pl.BlockSpec((TOK_BLOCK, hw), lambda i: (jnp.minimum(i, npb - 1), 0))
    sspec = pl.BlockSpec((TOK_BLOCK, hw), lambda i: (jnp.maximum(i - npb, 0), 0))
    n_grp = len(ATTN_GROUPS)
    return pl.pallas_call(
        functools.partial(_merge_kernel, n_prompt_blocks=npb),
        grid=(npb + nsb,),
        in_specs=[pspec] * (2 * n_grp) + [sspec] * (2 * n_grp),
        out_specs=pl.BlockSpec((TOK_BLOCK, hw), lambda i: (i, 0)),
        out_shape=jax.ShapeDtypeStruct((n_tok + n_seq, hw), BF16),
        compiler_params=_params(("parallel",)),
        name="merge_groups",
    )(*prompt_o, *prompt_lse, *sample_o, *sample_lse)


def _pool_kernel(*refs, rows, n_prompt_blocks, n_seq):
    ng = len(POOL_WINDOWS)
    cur_refs, halo_refs = refs[0:ng], refs[ng:2 * ng]
    st_ref, w_ref, s_ref, o_ref = refs[2 * ng:2 * ng + 4]
    i = pl.program_id(0)
    pg = cur_refs[0].shape[1]

    @pl.when(i < n_prompt_blocks)
    def _():
        pos = i * rows + lax.broadcasted_iota(jnp.int32, (rows, 1), 0)
        for g, win in enumerate(POOL_WINDOWS):
            cur = cur_refs[g][...]
            halo = jnp.where(i > 0, halo_refs[g][...], 0.0)
            a = jnp.concatenate([halo, cur], axis=0)
            sh = 1
            while sh < win:
                a = a + pltpu.roll(a, sh, 0)
                sh *= 2
            cnt = jnp.minimum(pos + 1, win).astype(F32)
            d = a[POOL_HALO:, :] / cnt - cur
            sl = slice(g * pg, (g + 1) * pg)
            y = jnp.dot(d.astype(BF16), w_ref[g].astype(BF16), preferred_element_type=F32)
            o_ref[:, sl] = (y * s_ref[:, sl]).astype(o_ref.dtype)

    @pl.when(i == n_prompt_blocks)
    def _():
        ctx = st_ref.shape[1]
        for g, win in enumerate(POOL_WINDOWS):
            sl = slice(g * pg, (g + 1) * pg)
            cur = cur_refs[g][0:n_seq, :]
            acc = cur
            for jj in range(1, win):
                acc = acc + st_ref[:, ctx - jj, sl]
            cnt = float(min(PAST_LEN + 1, win))
            d = acc / cnt - cur
            y = jnp.dot(d.astype(BF16), w_ref[g].astype(BF16), preferred_element_type=F32)
            o_ref[0:n_seq, sl] = (y * s_ref[:, sl]).astype(o_ref.dtype)


def _pool_branch(z, state, w_pool, pool_scale, n_tok, n_seq, u_col0):
    m = z.shape[0]
    ng, pg, _ = w_pool.shape
    pw = ng * pg
    rows = 256
    assert n_seq <= rows and m == n_tok + n_seq
    c0 = u_col0 // pg
    scale = pool_scale.reshape(1, pw)
    hpb = rows // POOL_HALO
    npb = n_tok // rows

    cur_specs = [pl.BlockSpec((rows, pg), lambda i, g=g: (i, c0 + g)) for g in range(ng)]
    halo_specs = [pl.BlockSpec((POOL_HALO, pg),
                               lambda i, g=g: (jnp.maximum(i * hpb - 1, 0), c0 + g))
                  for g in range(ng)]
    return pl.pallas_call(
        functools.partial(_pool_kernel, rows=rows, n_prompt_blocks=npb, n_seq=n_seq),
        grid=(npb + 1,),
        in_specs=cur_specs + halo_specs + [
            pl.BlockSpec(state.shape, lambda i: (0, 0, 0)),
            pl.BlockSpec((ng, pg, pg), lambda i: (0, 0, 0)),
            pl.BlockSpec((1, pw), lambda i: (0, 0))],
        out_specs=pl.BlockSpec((rows, pw), lambda i: (i, 0)),
        out_shape=jax.ShapeDtypeStruct((m, pw), BF16),
        compiler_params=_params(("parallel",)),
        name="pool",
    )(*([z] * (2 * ng)), state, w_pool, scale)


def _mix_kernel(oa_ref, po_ref, wa_ref, wp_ref, ga_ref, gb_ref, o_ref, wab_ref, wpb_ref):
    @pl.when(pl.program_id(1) == 0)
    def _():
        wab_ref[...] = wa_ref[...].astype(BF16)
        wpb_ref[...] = wp_ref[...].astype(BF16)

    a = jnp.dot(oa_ref[...], wab_ref[...], preferred_element_type=F32)
    p = jnp.dot(po_ref[...], wpb_ref[...], preferred_element_type=F32)
    o_ref[...] = (ga_ref[...] * a + gb_ref[...] * p).astype(o_ref.dtype)


def _mix(o_attn, pool_out, w_a, w_p, z, gate_col0):
    m, ka = o_attn.shape
    kp = pool_out.shape[1]
    n = w_a.shape[1]
    ga0 = gate_col0 // COL_TILE
    gb0 = (gate_col0 + n) // COL_TILE
    return pl.pallas_call(
        _mix_kernel,
        grid=(n // COL_TILE, m // ROW_TILE),
        in_specs=[pl.BlockSpec((ROW_TILE, ka), lambda j, i: (i, 0)),
                  pl.BlockSpec((ROW_TILE, kp), lambda j, i: (i, 0)),
                  pl.BlockSpec((ka, COL_TILE), lambda j, i: (0, j)),
                  pl.BlockSpec((kp, COL_TILE), lambda j, i: (0, j)),
                  pl.BlockSpec((ROW_TILE, COL_TILE), lambda j, i: (i, ga0 + j)),
                  pl.BlockSpec((ROW_TILE, COL_TILE), lambda j, i: (i, gb0 + j))],
        out_specs=pl.BlockSpec((ROW_TILE, COL_TILE), lambda j, i: (i, j)),
        out_shape=jax.ShapeDtypeStruct((m, n), BF16),
        scratch_shapes=[pltpu.VMEM((ka, COL_TILE), BF16), pltpu.VMEM((kp, COL_TILE), BF16)],
        compiler_params=_params(("arbitrary", "arbitrary")),
        name="mix",
    )(o_attn, pool_out, w_a, w_p, z, z)


def _outproj_kernel(a_ref, w_ref, xp_ref, xs_ref, o_ref, wb_ref, *, split):
    i = pl.program_id(1)
    last = pl.num_programs(1) - 1

    @pl.when(i == 0)
    def _():
        wb_ref[...] = w_ref[...].astype(BF16)

    y = jnp.dot(a_ref[...], wb_ref[...], preferred_element_type=F32)

    @pl.when(i < last)
    def _():
        o_ref[...] = xp_ref[...] + y

    @pl.when(i == last)
    def _():
        o_ref[0:split, :] = xp_ref[0:split, :] + y[0:split, :]
        o_ref[split:, :] = xs_ref[...] + y[split:, :]


def _outproj(mixed, w_out, xp, xs):
    m, k = mixed.shape
    n = w_out.shape[1]
    n_tok, n_seq = xp.shape[0], xs.shape[0]
    nblk = m // ROW_TILE
    split = n_tok - (nblk - 1) * ROW_TILE
    assert 0 < split and split + n_seq == ROW_TILE and split % 8 == 0
    return pl.pallas_call(
        functools.partial(_outproj_kernel, split=split),
        grid=(n // COL_TILE, nblk),
        in_specs=[pl.BlockSpec((ROW_TILE, k), lambda j, i: (i, 0)),
                  pl.BlockSpec((k, COL_TILE), lambda j, i: (0, j)),
                  pl.BlockSpec((ROW_TILE, COL_TILE), lambda j, i: (i, j)),
                  pl.BlockSpec((n_seq, COL_TILE), lambda j, i: (0, j))],
        out_specs=pl.BlockSpec((ROW_TILE, COL_TILE), lambda j, i: (i, j)),
        out_shape=jax.ShapeDtypeStruct((m, n), F32),
        scratch_shapes=[pltpu.VMEM((k, COL_TILE), BF16)],
        compiler_params=_params(("arbitrary", "arbitrary")),
        name="outproj",
    )(mixed, w_out, xp, xs)


def _router_kernel(x_ref, g_ref, wr_ref, br_ref, h_ref, e_ref, r_ref, w_ref, c_ref, carry_ref):
    i = pl.program_id(0)

    @pl.when(i == 0)
    def _():
        carry_ref[...] = jnp.zeros_like(carry_ref)

    x = x_ref[...]
    ms = jnp.mean(x * x, axis=-1, keepdims=True)
    h = x * lax.rsqrt(ms + RMS_EPS) * g_ref[...]
    h_ref[...] = h
    logits = jnp.dot(h.astype(BF16), wr_ref[...].astype(BF16),
                     preferred_element_type=F32) + br_ref[...]
    tb, ne = logits.shape
    elane = lax.broadcasted_iota(jnp.int32, (tb, ne), 1).astype(F32)
    work = logits
    sel = jnp.zeros((tb, ne), F32)
    idxs, vals = [], []
    for _ in range(TOP_K):
        mx = jnp.max(work, axis=-1, keepdims=True)
        idx = jnp.min(jnp.where(work == mx, elane, float(ne)), axis=-1, keepdims=True)
        hit = elane == idx
        sel = jnp.where(hit, 1.0, sel)
        work = jnp.where(hit, -jnp.inf, work)
        idxs.append(idx)
        vals.append(mx)
    exps = [jnp.exp(v - vals[0]) for v in vals]
    den = functools.reduce(lambda a, b: a + b, exps)

    ri = lax.broadcasted_iota(jnp.int32, (tb, tb), 0)
    ci = lax.broadcasted_iota(jnp.int32, (tb, tb), 1)
    tril = jnp.where(ci < ri, 1.0, 0.0).astype(BF16)
    before = jnp.dot(tril, sel.astype(BF16), preferred_element_type=F32) + carry_ref[0:1, :]
    carry_ref[...] = carry_ref[...] + jnp.sum(sel, axis=0, keepdims=True)

    lane = lax.broadcasted_iota(jnp.int32, (tb, HEAD_DIM), 1)
    e_out = jnp.zeros((tb, HEAD_DIM), jnp.int32)
    r_out = jnp.zeros((tb, HEAD_DIM), jnp.int32)
    w_out = jnp.zeros((tb, HEAD_DIM), F32)
    for k in range(TOP_K):
        rank = jnp.sum(jnp.where(elane == idxs[k], before, 0.0), axis=-1, keepdims=True)
        e_out = jnp.where(lane == k, idxs[k].astype(jnp.int32), e_out)
        r_out = jnp.where(lane == k, rank.astype(jnp.int32), r_out)
        w_out = jnp.where(lane == k, exps[k] / den, w_out)
    e_ref[...] = e_out
    r_ref[...] = r_out
    w_ref[...] = w_out
    c_ref[...] = carry_ref[...].astype(jnp.int32)


def _router(x1, gain, w_router, b_router):
    m, d = x1.shape
    ne = w_router.shape[1]
    nblk = m // TOK_BLOCK
    lanes = HEAD_DIM
    return pl.pallas_call(
        _router_kernel,
        grid=(nblk,),
        in_specs=[pl.BlockSpec((TOK_BLOCK, d), lambda i: (i, 0)),
                  pl.BlockSpec((1, d), lambda i: (0, 0)),
                  pl.BlockSpec((d, ne), lambda i: (0, 0)),
                  pl.BlockSpec((1, ne), lambda i: (0, 0))],
        out_specs=[pl.BlockSpec((TOK_BLOCK, d), lambda i: (i, 0)),
                   pl.BlockSpec((TOK_BLOCK, lanes), lambda i: (i, 0)),
                   pl.BlockSpec((TOK_BLOCK, lanes), lambda i: (i, 0)),
                   pl.BlockSpec((TOK_BLOCK, lanes), lambda i: (i, 0)),
                   pl.BlockSpec((8, ne), lambda i: (0, 0))],
        out_shape=[jax.ShapeDtypeStruct((m, d), F32),
                   jax.ShapeDtypeStruct((m, lanes), jnp.int32),
                   jax.ShapeDtypeStruct((m, lanes), jnp.int32),
                   jax.ShapeDtypeStruct((m, lanes), F32),
                   jax.ShapeDtypeStruct((8, ne), jnp.int32)],
        scratch_shapes=[pltpu.VMEM((8, ne), F32)],
        compiler_params=_params(("arbitrary",)),
        name="router",
    )(x1, gain.reshape(1, d), w_router, b_router.reshape(1, ne))


def _dispatch_kernel(tok_ref, h_hbm, o_ref, buf_ref, sem):
    rows = buf_ref.shape[0]

    def row_copy(r):
        return pltpu.make_async_copy(h_hbm.at[pl.ds(tok_ref[0, 0, r], 1), :],
                                     buf_ref.at[pl.ds(r, 1), :], sem)

    def start(r2, c):
        row_copy(2 * r2).start(priority=0)
        row_copy(2 * r2 + 1).start(priority=1)
        return c

    def wait(r, c):
        row_copy(r).wait()
        return c

    lax.fori_loop(0, rows // 2, start, 0)
    lax.fori_loop(0, rows, wait, 0)
    o_ref[...] = buf_ref[...].astype(o_ref.dtype)


def _dispatch(h, row_tok, n_rows):
    d = h.shape[1]
    nu = n_rows // EXPERT_TILE
    return pl.pallas_call(
        _dispatch_kernel,
        grid=(nu,),
        in_specs=[pl.BlockSpec((1, 1, EXPERT_TILE), lambda u: (u, 0, 0),
                               memory_space=pltpu.SMEM),
                  pl.BlockSpec(memory_space=pl.ANY)],
        out_specs=pl.BlockSpec((EXPERT_TILE, d), lambda u: (u, 0)),
        out_shape=jax.ShapeDtypeStruct((n_rows, d), BF16),
        scratch_shapes=[pltpu.VMEM((EXPERT_TILE, d), F32), pltpu.SemaphoreType.DMA(())],
        compiler_params=_params(("arbitrary",)),
        name="dispatch",
    )(row_tok.reshape(nu, 1, EXPERT_TILE), h)


def _stream_weight_tiles(w_hbm, col_offsets, wbuf, wbf_refs, sem, count_ref,
                         ue_ref, uf_ref, nxe_ref, nxw_ref):
    f = pl.program_id(0)
    u = pl.program_id(1)
    nf = pl.num_programs(0)
    parts = range(len(col_offsets))

    def tile_copy(e, fcol, slot, part):
        col = pl.multiple_of(fcol * COL_TILE + col_offsets[part], COL_TILE)
        return pltpu.make_async_copy(w_hbm.at[0, e, :, pl.ds(col, COL_TILE)],
                                     wbuf.at[slot, part], sem.at[slot, part])

    @pl.when(jnp.logical_and(f == 0, u == 0))
    def _():
        count_ref[0] = 0
        for part in parts:
            tile_copy(ue_ref[0], 0, 0, part).start()

    @pl.when(uf_ref[u] == 1)
    def _():
        t = count_ref[0]
        slot = lax.rem(t, 2)
        for part in parts:
            tile_copy(ue_ref[u], f, slot, part).wait()
        wraps = nxw_ref[u]

        @pl.when(jnp.logical_not(jnp.logical_and(wraps == 1, f == nf - 1)))
        def _():
            for part in parts:
                tile_copy(nxe_ref[u], f + wraps, 1 - slot, part).start()

        for part in parts:
            wbf_refs[part][...] = wbuf[slot, part].astype(BF16)
        count_ref[0] = t + 1


def _gate_up_kernel(ue_ref, ub_ref, uf_ref, nxe_ref, nxw_ref, na_ref, x_ref, w_hbm, bg_ref,
                    bu_ref, o_ref, wbuf, wgb_ref, wub_ref, sem, count_ref):
    del ub_ref
    u = pl.program_id(1)
    ff = pl.num_programs(0) * COL_TILE
    _stream_weight_tiles(w_hbm, (0, ff), wbuf, (wgb_ref, wub_ref), sem, count_ref,
                         ue_ref, uf_ref, nxe_ref, nxw_ref)

    @pl.when(u < na_ref[0])
    def _():
        x = x_ref[...]
        gate = jnp.dot(x, wgb_ref[...], preferred_element_type=F32) + bg_ref[...]
        up = jnp.dot(x, wub_ref[...], preferred_element_type=F32) + bu_ref[...]
        gate = jnp.minimum(gate, SWIGLU_LIMIT)
        up = jnp.clip(up, -SWIGLU_LIMIT, SWIGLU_LIMIT)
        act = gate * jax.nn.sigmoid(SWIGLU_ALPHA * gate) * (up + 1.0)
        o_ref[...] = act.astype(o_ref.dtype)

    @pl.when(u >= na_ref[0])
    def _():
        o_ref[...] = jnp.zeros_like(o_ref)


def _unit_imap(fn):
    return lambda c, u, ue, ub, uf, nxe, nxw, na: fn(c, u, ue, ub)


def _gate_up(xs, w_gate_up, b_gate_up, units):
    n_rows, d = xs.shape
    ne = w_gate_up.shape[1]
    ff = w_gate_up.shape[3] // 2
    nu = n_rows // EXPERT_TILE
    nf = ff // COL_TILE
    bias = b_gate_up.reshape(ne, 1, 2 * ff)
    grid_spec = pltpu.PrefetchScalarGridSpec(
        num_scalar_prefetch=len(units),
        grid=(nf, nu),
        in_specs=[
            pl.BlockSpec((EXPERT_TILE, d), _unit_imap(lambda f, u, ue, ub: (ub[u], 0))),
            pl.BlockSpec(memory_space=pl.ANY),
            pl.BlockSpec((None, 1, COL_TILE), _unit_imap(lambda f, u, ue, ub: (ue[u], 0, f))),
            pl.BlockSpec((None, 1, COL_TILE),
                         _unit_imap(lambda f, u, ue, ub: (ue[u], 0, nf + f))),
        ],
        out_specs=pl.BlockSpec((EXPERT_TILE, COL_TILE),
                               _unit_imap(lambda f, u, ue, ub: (u, f))),
        scratch_shapes=[pltpu.VMEM((2, 2, d, COL_TILE), F32),
                        pltpu.VMEM((d, COL_TILE), BF16), pltpu.VMEM((d, COL_TILE), BF16),
                        pltpu.SemaphoreType.DMA((2, 2)), pltpu.SMEM((1,), jnp.int32)],
    )
    return pl.pallas_call(
        _gate_up_kernel,
        grid_spec=grid_spec,
        out_shape=jax.ShapeDtypeStruct((n_rows, ff), BF16),
        compiler_params=_params(("arbitrary", "arbitrary")),
        name="expert_gate_up",
    )(*units, xs, w_gate_up, bias, bias)


def _down_kernel(ue_ref, ub_ref, uf_ref, nxe_ref, nxw_ref, na_ref, a_ref, w_hbm, b_ref,
                 o_ref, wbuf, wb_ref, sem, count_ref):
    del ub_ref
    u = pl.program_id(1)
    _stream_weight_tiles(w_hbm, (0,), wbuf, (wb_ref,), sem, count_ref,
                         ue_ref, uf_ref, nxe_ref, nxw_ref)

    @pl.when(u < na_ref[0])
    def _():
        o_ref[...] = jnp.dot(a_ref[...], wb_ref[...], preferred_element_type=F32) + b_ref[...]

    @pl.when(u >= na_ref[0])
    def _():
        o_ref[...] = jnp.zeros_like(o_ref)


def _down(act, w_down, b_down, units):
    n_rows, ff = act.shape
    ne = w_down.shape[1]
    d = w_down.shape[3]
    nu = n_rows // EXPERT_TILE
    bias = b_down.reshape(ne, 1, d)
    grid_spec = pltpu.PrefetchScalarGridSpec(
        num_scalar_prefetch=len(units),
        grid=(d // COL_TILE, nu),
        in_specs=[
            pl.BlockSpec((EXPERT_TILE, ff), _unit_imap(lambda n, u, ue, ub: (ub[u], 0))),
            pl.BlockSpec(memory_space=pl.ANY),
            pl.BlockSpec((None, 1, COL_TILE), _unit_imap(lambda n, u, ue, ub: (ue[u], 0, n))),
        ],
        out_specs=pl.BlockSpec((EXPERT_TILE, COL_TILE),
                               _unit_imap(lambda n, u, ue, ub: (u, n))),
        scratch_shapes=[pltpu.VMEM((2, 1, ff, COL_TILE), F32),
                        pltpu.VMEM((ff, COL_TILE), BF16),
                        pltpu.SemaphoreType.DMA((2, 1)), pltpu.SMEM((1,), jnp.int32)],
    )
    return pl.pallas_call(
        _down_kernel,
        grid_spec=grid_spec,
        out_shape=jax.ShapeDtypeStruct((n_rows, d), F32),
        compiler_params=_params(("arbitrary", "arbitrary")),
        name="expert_down",
    )(*units, act, w_down, bias)


def _combine_kernel(pos_ref, x_ref, w_ref, y_hbm, op_ref, os_ref, buf_ref, sem, *,
                    n_prompt_blocks):
    tb = x_ref.shape[0]
    i = pl.program_id(0)

    def row_copy(t, k):
        return pltpu.make_async_copy(y_hbm.at[pl.ds(pos_ref[0, 0, t * TOP_K + k], 1), :],
                                     buf_ref.at[k, pl.ds(t, 1), :], sem)

    def start(t, c):
        for k in range(TOP_K):
            row_copy(t, k).start(priority=k % 2)
        return c

    def wait(t, c):
        for k in range(TOP_K):
            row_copy(t, k).wait()
        return c

    lax.fori_loop(0, tb, start, 0)
    lax.fori_loop(0, tb, wait, 0)
    w = w_ref[...]
    acc = w[:, 0:1] * buf_ref[0]
    for k in range(1, TOP_K):
        acc = acc + w[:, k:k + 1] * buf_ref[k]
    y = x_ref[...] + acc

    @pl.when(i < n_prompt_blocks)
    def _():
        op_ref[...] = y

    @pl.when(i >= n_prompt_blocks)
    def _():
        os_ref[...] = y


def _combine(x1, wts, pos, y_sorted, n_tok, n_seq):
    m, d = x1.shape
    nblk = m // TOK_BLOCK
    npb = n_tok // TOK_BLOCK
    assert n_seq == TOK_BLOCK and npb + 1 == nblk
    return pl.pallas_call(
        functools.partial(_combine_kernel, n_prompt_blocks=npb),
        grid=(nblk,),
        in_specs=[pl.BlockSpec((1, 1, TOK_BLOCK * TOP_K), lambda i: (i, 0, 0),
                               memory_space=pltpu.SMEM),
                  pl.BlockSpec((TOK_BLOCK, d), lambda i: (i, 0)),
                  pl.BlockSpec((TOK_BLOCK, wts.shape[1]), lambda i: (i, 0)),
                  pl.BlockSpec(memory_space=pl.ANY)],
        out_specs=[pl.BlockSpec((TOK_BLOCK, d), lambda i: (jnp.minimum(i, npb - 1), 0)),
                   pl.BlockSpec((TOK_BLOCK, d), lambda i: (0, 0))],
        out_shape=[jax.ShapeDtypeStruct((n_tok, d), F32),
                   jax.ShapeDtypeStruct((n_seq, d), F32)],
        scratch_shapes=[pltpu.VMEM((TOP_K, TOK_BLOCK, d), F32), pltpu.SemaphoreType.DMA(())],
        compiler_params=_params(("arbitrary",)),
        name="combine",
    )(pos.reshape(nblk, 1, TOK_BLOCK * TOP_K), x1, wts, y_sorted)


def _moe(x1, layer_w, n_tok):
    norm_ffn, w_router, b_router, w_gate_up, b_gate_up, w_down, b_down = layer_w
    m = x1.shape[0]
    ne = w_router.shape[1]
    h, eidx, rank, wts, counts = _router(x1, norm_ffn, w_router, b_router)

    tile = EXPERT_TILE
    counts = counts[0]
    padded = (counts + tile - 1) // tile * tile
    pend = jnp.cumsum(padded)
    pstart = pend - padded
    eidx = eidx[:, :TOP_K]
    pos = (pstart[eidx] + rank[:, :TOP_K]).astype(jnp.int32)
    n_rows = (m * TOP_K + ne * (tile - 1) + tile - 1) // tile * tile
    nu = n_rows // tile
    tok = jnp.broadcast_to(jnp.arange(m, dtype=jnp.int32)[:, None], pos.shape)
    row_tok = jnp.zeros((n_rows,), jnp.int32).at[pos.reshape(-1)].set(tok.reshape(-1))
    n_active = (pend[-1] // tile).astype(jnp.int32)
    u = jnp.arange(nu, dtype=jnp.int32)
    unit_blk = jnp.minimum(u, n_active - 1)
    unit_e = jnp.sum((pend[None, :] <= (unit_blk * tile)[:, None]).astype(jnp.int32), axis=1)
    unit_e = jnp.minimum(unit_e, ne - 1)
    prev_e = jnp.concatenate([jnp.full((1,), -1, jnp.int32), unit_e[:-1]])
    first = (u < n_active) & (unit_e != prev_e)
    unit_first = first.astype(jnp.int32)
    later = first[None, :] & (u[None, :] > u[:, None])
    nxt = jnp.min(jnp.where(later, u[None, :], nu), axis=1)
    next_wraps = (nxt >= nu).astype(jnp.int32)
    next_e = jnp.where(nxt >= nu, unit_e[0], unit_e[jnp.minimum(nxt, nu - 1)])
    units = (unit_e, unit_blk, unit_first, next_e.astype(jnp.int32), next_wraps,
             n_active.reshape(1))

    xs = _dispatch(h, row_tok, n_rows)
    act = _gate_up(xs, w_gate_up, b_gate_up, units)
    ys = _down(act, w_down, b_down, units)
    return _combine(x1, wts, pos, ys, n_tok, m - n_tok)


def kernel(x_prompt, x_sample, cache_kv_w128, cache_kv_w512, cache_kv_w2048, state_pool,
           norm_mix, w_in, q_norm, k_norm, w_pool_group, pool_scale, w_branch_attn,
           w_branch_pool, w_out, norm_ffn, w_router, b_router, w_gate_up, b_gate_up,
           w_down, b_down):
    depth = norm_mix.shape[0]
    assert depth == 1 and x_prompt.shape[0] == 1 and x_sample.shape[1] == 1
    n_tok, d_model = x_prompt.shape[1], x_prompt.shape[2]
    n_seq = x_sample.shape[0]
    caches = (cache_kv_w128, cache_kv_w512, cache_kv_w2048)
    n_grp = len(ATTN_GROUPS)
    hw = HEADS * HEAD_DIM
    qkv_w = n_grp * hw
    pool_w = w_pool_group.shape[1] * w_pool_group.shape[2]
    width = w_in.shape[2]
    layer = 0

    xp, xs = x_prompt[0], x_sample[:, 0]

    ones = jnp.ones((width - 2 * qkv_w,), F32)
    colgain = jnp.concatenate([
        jnp.broadcast_to(q_norm[layer][:, None, :], (n_grp, HEADS, HEAD_DIM)).reshape(-1),
        jnp.broadcast_to(k_norm[layer][:, None, :], (n_grp, HEADS, HEAD_DIM)).reshape(-1),
        ones]).reshape(1, width)

    h = _rmsnorm(xp, xs, norm_mix[layer])
    z = _inproj(h, w_in[layer], colgain, 2 * qkv_w, qkv_w + pool_w)

    po, plse, so, slse = [], [], [], []
    for g, (window, dil) in enumerate(ATTN_GROUPS):
        assert window == BAND * dil
        o, lse = _attn_prompt(z, g, dil, n_tok)
        po.append(o)
        plse.append(lse)
        o, lse = _attn_sample(z, caches[g][layer], g, dil, n_tok, n_seq)
        so.append(o)
        slse.append(lse)
    o_attn = _merge_groups(po, plse, so, slse)

    pool_out = _pool_branch(z, state_pool[layer], w_pool_group[layer], pool_scale[layer],
                            n_tok, n_seq, 3 * qkv_w)
    mixed = _mix(o_attn, pool_out, w_branch_attn[layer], w_branch_pool[layer], z,
                 3 * qkv_w + pool_w)
    x1 = _outproj(mixed, w_out[layer], xp, xs)
    y_p, y_s = _moe(x1, (norm_ffn[layer], w_router[layer], b_router[layer],
                         w_gate_up, b_gate_up, w_down, b_down), n_tok)
    y_prompt = y_p[None]
    y_sample = y_s[:, None]

    outs = [y_prompt, y_sample]
    k_all = z[:, qkv_w:2 * qkv_w]
    v_all = z[:, 2 * qkv_w:3 * qkv_w]
    for g, (window, dil) in enumerate(ATTN_GROUPS):
        keep = min(window, n_tok)
        sl = slice(g * hw, (g + 1) * hw)
        kp = k_all[n_tok - keep:n_tok, sl].reshape(keep, HEADS, HEAD_DIM)
        vp = v_all[n_tok - keep:n_tok, sl].reshape(keep, HEADS, HEAD_DIM)
        outs.append(jnp.stack([kp, vp], axis=0)[None, None])
        ks = k_all[n_tok:, sl].reshape(n_seq, 1, HEADS, HEAD_DIM)
        vs = v_all[n_tok:, sl].reshape(n_seq, 1, HEADS, HEAD_DIM)
        outs.append(jnp.stack([ks, vs], axis=1)[None])
    u_all = z[:, 3 * qkv_w:3 * qkv_w + pool_w]
    keep = min(POOL_HALO - 1, n_tok)
    outs.append(u_all[n_tok - keep:n_tok][None, None])
    outs.append(u_all[n_tok:][:, None][None])
    return tuple(outs)
```

```python
import functools

import jax
import jax.numpy as jnp
from jax import lax
from jax.experimental import pallas as pl
from jax.experimental.pallas import tpu as pltpu

F32 = jnp.float32
BF16 = jnp.bfloat16

HEAD_DIM = 128
HEADS = 8
ATTN_GROUPS = ((128, 1), (512, 4), (2048, 16))
BAND = 128
ATTN_ROWS = 2048
POOL_WINDOWS = (2, 4, 8, 16)
POOL_HALO = 16
TOP_K = 4
SWIGLU_LIMIT = 7.0
SWIGLU_ALPHA = 1.702
RMS_EPS = 1e-5
NEG_INF = -1e30
PAST_LEN = 2048

ROW_TILE = 640
COL_TILE = 512
TOK_BLOCK = 128
EXPERT_TILE = 256
EXPERT_COLS = 1024
CAST_ROWS = 256
VMEM_LIMIT = 56 * 1024 * 1024
EXPERT_VMEM_LIMIT = 62 * 1024 * 1024


def _params(semantics, vmem=VMEM_LIMIT):
    return pltpu.CompilerParams(dimension_semantics=semantics, vmem_limit_bytes=vmem)


def _rmsnorm_kernel(xp_ref, xs_ref, g_ref, o_ref, *, n_prompt_blocks):
    i = pl.program_id(0)

    def norm(x_ref):
        x = x_ref[...]
        ms = jnp.mean(x * x, axis=-1, keepdims=True)
        o_ref[...] = (x * lax.rsqrt(ms + RMS_EPS) * g_ref[...]).astype(o_ref.dtype)

    @pl.when(i < n_prompt_blocks)
    def _():
        norm(xp_ref)

    @pl.when(i >= n_prompt_blocks)
    def _():
        norm(xs_ref)


def _rmsnorm(xp, xs, gain):
    n_tok, d = xp.shape
    n_seq = xs.shape[0]
    npb, nsb = n_tok // TOK_BLOCK, n_seq // TOK_BLOCK
    return pl.pallas_call(
        functools.partial(_rmsnorm_kernel, n_prompt_blocks=npb),
        grid=(npb + nsb,),
        in_specs=[pl.BlockSpec((TOK_BLOCK, d), lambda i: (jnp.minimum(i, npb - 1), 0)),
                  pl.BlockSpec((TOK_BLOCK, d), lambda i: (jnp.maximum(i - npb, 0), 0)),
                  pl.BlockSpec((1, d), lambda i: (0, 0))],
        out_specs=pl.BlockSpec((TOK_BLOCK, d), lambda i: (i, 0)),
        out_shape=jax.ShapeDtypeStruct((n_tok + n_seq, d), BF16),
        compiler_params=_params(("parallel",)),
        name="rmsnorm",
    )(xp, xs, gain.reshape(1, d))


def _inproj_kernel(h_ref, w_ref, cg_ref, o_ref, wb_ref, *, qk_tiles, plain_end):
    j = pl.program_id(0)
    i = pl.program_id(1)

    @pl.when(i == 0)
    def _():
        wb_ref[...] = w_ref[...].astype(BF16)

    acc = jnp.dot(h_ref[...], wb_ref[...], preferred_element_type=F32)

    @pl.when(j < qk_tiles)
    def _():
        for hh in range(acc.shape[1] // HEAD_DIM):
            sl = slice(hh * HEAD_DIM, (hh + 1) * HEAD_DIM)
            blk = acc[:, sl]
            ms = jnp.mean(blk * blk, axis=-1, keepdims=True)
            o_ref[:, sl] = blk * lax.rsqrt(ms + RMS_EPS) * cg_ref[:, sl]

    @pl.when(jnp.logical_and(j >= qk_tiles, j < plain_end))
    def _():
        o_ref[...] = acc

    @pl.when(j >= plain_end)
    def _():
        o_ref[...] = jax.nn.sigmoid(acc)


def _inproj(h, w_in, colgain, qk_width, plain_width):
    m, d = h.shape
    n = w_in.shape[1]
    kern = functools.partial(_inproj_kernel, qk_tiles=qk_width // COL_TILE,
                             plain_end=(qk_width + plain_width) // COL_TILE)
    return pl.pallas_call(
        kern,
        grid=(n // COL_TILE, m // ROW_TILE),
        in_specs=[pl.BlockSpec((ROW_TILE, d), lambda j, i: (i, 0)),
                  pl.BlockSpec((d, COL_TILE), lambda j, i: (0, j)),
                  pl.BlockSpec((1, COL_TILE), lambda j, i: (0, j))],
        out_specs=pl.BlockSpec((ROW_TILE, COL_TILE), lambda j, i: (i, j)),
        out_shape=jax.ShapeDtypeStruct((m, n), F32),
        scratch_shapes=[pltpu.VMEM((d, COL_TILE), BF16)],
        compiler_params=_params(("arbitrary", "arbitrary")),
        name="inproj",
    )(h, w_in, colgain)


def _attn_prompt_kernel(q_ref, kh_ref, k_ref, vh_ref, v_ref, o_ref, lse_ref, *, dil):
    n = pl.program_id(1)
    span = BAND * dil
    qi = lax.broadcasted_iota(jnp.int32, (BAND, 2 * BAND), 0) + BAND
    ki = lax.broadcasted_iota(jnp.int32, (BAND, 2 * BAND), 1)
    dist = qi - ki
    band = (dist >= 0) & (dist <= BAND)
    first_key = jnp.where(n > 0, 0, BAND)
    band_first = band & (ki >= first_key)
    scale = HEAD_DIM ** -0.5

    def rows(ref, start, size):
        if dil == 1:
            return ref[pl.ds(start, size), :]
        return ref[pl.ds(start, size, stride=dil), :]

    for c in range(ATTN_ROWS // span):
        for r in range(dil):
            base = c * span + r
            q = rows(q_ref, base, BAND).astype(BF16)
            if c == 0:
                kc = jnp.concatenate([rows(kh_ref, r, BAND), rows(k_ref, r, BAND)], axis=0)
                vc = jnp.concatenate([rows(vh_ref, r, BAND), rows(v_ref, r, BAND)], axis=0)
                valid = band_first
            else:
                kc = rows(k_ref, base - span, 2 * BAND)
                vc = rows(v_ref, base - span, 2 * BAND)
                valid = band
            s = lax.dot_general(q, kc.astype(BF16), (((1,), (1,)), ((), ())),
                                preferred_element_type=F32) * scale
            s = jnp.where(valid, s, NEG_INF)
            m = jnp.max(s, axis=-1, keepdims=True)
            p = jnp.exp(s - m)
            l = jnp.sum(p, axis=-1, keepdims=True)
            o = jnp.dot(p.astype(BF16), vc.astype(BF16), preferred_element_type=F32) / l
            lse = jnp.broadcast_to(m + jnp.log(l), (BAND, HEAD_DIM))
            if dil == 1:
                o_ref[pl.ds(base, BAND), :] = o
                lse_ref[pl.ds(base, BAND), :] = lse
            else:
                o_ref[pl.ds(base, BAND, stride=dil), :] = o
                lse_ref[pl.ds(base, BAND, stride=dil), :] = lse


def _attn_prompt(z, g, dil, n_tok):
    hw = HEADS * HEAD_DIM
    n_grp = len(ATTN_GROUPS)
    span = BAND * dil
    spb = ATTN_ROWS // span
    qcol, kcol, vcol = (g * HEADS, (n_grp + g) * HEADS, (2 * n_grp + g) * HEADS)

    def cur(col):
        return pl.BlockSpec((ATTN_ROWS, HEAD_DIM), lambda h, n: (n, col + h))

    def halo(col):
        return pl.BlockSpec((span, HEAD_DIM),
                            lambda h, n: (jnp.maximum(n * spb - 1, 0), col + h))

    out_spec = pl.BlockSpec((ATTN_ROWS, HEAD_DIM), lambda h, n: (n, h))
    return pl.pallas_call(
        functools.partial(_attn_prompt_kernel, dil=dil),
        grid=(HEADS, n_tok // ATTN_ROWS),
        in_specs=[cur(qcol), halo(kcol), cur(kcol), halo(vcol), cur(vcol)],
        out_specs=[out_spec, out_spec],
        out_shape=[jax.ShapeDtypeStruct((n_tok, hw), F32)] * 2,
        compiler_params=_params(("parallel", "parallel")),
        name=f"attn_prompt_d{dil}",
    )(z, z, z, z, z)


def _attn_sample_kernel(q_ref, kn_ref, vn_ref, kv_ref, o_ref, lse_ref, *, rows):
    scale = HEAD_DIM ** -0.5

    def to_heads(row):
        return jnp.concatenate(
            [row[:, h * HEAD_DIM:(h + 1) * HEAD_DIM] for h in range(HEADS)], axis=0)

    def to_row(x):
        return jnp.concatenate([x[h:h + 1, :] for h in range(HEADS)], axis=1)

    o_rows, lse_rows = [], []
    for b in range(rows):
        q = to_heads(q_ref[b:b + 1, :])
        kn = to_heads(kn_ref[b:b + 1, :])
        vn = to_heads(vn_ref[b:b + 1, :])
        k = kv_ref[b, 0]
        v = kv_ref[b, 1]
        s = jnp.sum(k * q[None], axis=-1, keepdims=True) * scale
        sn = jnp.sum(q * kn, axis=-1, keepdims=True) * scale
        m = jnp.maximum(jnp.max(s, axis=0), sn)
        p = jnp.exp(s - m[None])
        pn = jnp.exp(sn - m)
        l = jnp.sum(p, axis=0) + pn
        o = (jnp.sum(p * v, axis=0) + pn * vn) / l
        lse = jnp.broadcast_to(m + jnp.log(l), (HEADS, HEAD_DIM))
        o_rows.append(to_row(o))
        lse_rows.append(to_row(lse))
    o_ref[...] = jnp.concatenate(o_rows, axis=0)
    lse_ref[...] = jnp.concatenate(lse_rows, axis=0)


def _attn_sample(z, cache, g, dil, row0, n_seq):
    hw = HEADS * HEAD_DIM
    n_grp = len(ATTN_GROUPS)
    length = cache.shape[2]
    assert length == BAND * dil, "cache must hold exactly one window"
    kv = cache.reshape(n_seq, 2, BAND, dil, HEADS, HEAD_DIM)
    rows = 8
    blk0 = row0 // rows

    def zspec(col):
        return pl.BlockSpec((rows, hw), lambda i: (blk0 + i, col))

    out_spec = pl.BlockSpec((rows, hw), lambda i: (i, 0))
    return pl.pallas_call(
        functools.partial(_attn_sample_kernel, rows=rows),
        grid=(n_seq // rows,),
        in_specs=[zspec(g), zspec(n_grp + g), zspec(2 * n_grp + g),
                  pl.BlockSpec((rows, 2, BAND, None, HEADS, HEAD_DIM),
                               lambda i: (i, 0, 0, 0, 0, 0))],
        out_specs=[out_spec, out_spec],
        out_shape=[jax.ShapeDtypeStruct((n_seq, hw), F32)] * 2,
        compiler_params=_params(("parallel",)),
        name=f"attn_sample_d{dil}",
    )(z, z, z, kv)


def _merge_kernel(*refs, n_prompt_blocks):
    n_grp = len(ATTN_GROUPS)
    po, pl_, so, sl_ = (refs[0:n_grp], refs[n_grp:2 * n_grp],
                        refs[2 * n_grp:3 * n_grp], refs[3 * n_grp:4 * n_grp])
    out_ref = refs[4 * n_grp]
    i = pl.program_id(0)

    def merge(o_refs, l_refs):
        ls = [r[...] for r in l_refs]
        mx = functools.reduce(jnp.maximum, ls)
        es = [jnp.exp(l - mx) for l in ls]
        den = functools.reduce(lambda a, b: a + b, es)
        acc = (es[0] / den) * o_refs[0][...]
        for g in range(1, n_grp):
            acc = acc + (es[g] / den) * o_refs[g][...]
        out_ref[...] = acc.astype(out_ref.dtype)

    @pl.when(i < n_prompt_blocks)
    def _():
        merge(po, pl_)

    @pl.when(i >= n_prompt_blocks)
    def _():
        merge(so, sl_)


def _merge_groups(prompt_o, prompt_lse, sample_o, sample_lse):
    n_tok = prompt_o[0].shape[0]
    n_seq = sample_o[0].shape[0]
    hw = HEADS * HEAD_DIM
    npb = n_tok // TOK_BLOCK
    nsb = n_seq // TOK_BLOCK

    pspec = pl.BlockSpec((TOK_BLOCK, hw), lambda i: (jnp.minimum(i, npb - 1), 0))
    sspec = pl.BlockSpec((TOK_BLOCK, hw), lambda i: (jnp.maximum(i - npb, 0), 0))
    n_grp = len(ATTN_GROUPS)
    return pl.pallas_call(
        functools.partial(_merge_kernel, n_prompt_blocks=npb),
        grid=(npb + nsb,),
        in_specs=[pspec] * (2 * n_grp) + [sspec] * (2 * n_grp),
        out_specs=pl.BlockSpec((TOK_BLOCK, hw), lambda i: (i, 0)),
        out_shape=jax.ShapeDtypeStruct((n_tok + n_seq, hw), BF16),
        compiler_params=_params(("parallel",)),
        name="merge_groups",
    )(*prompt_o, *prompt_lse, *sample_o, *sample_lse)


def _pool_kernel(*refs, rows, n_prompt_blocks, n_seq):
    ng = len(POOL_WINDOWS)
    cur_refs, halo_refs = refs[0:ng], refs[ng:2 * ng]
    st_ref, w_ref, s_ref, o_ref = refs[2 * ng:2 * ng + 4]
    i = pl.program_id(0)
    pg = cur_refs[0].shape[1]

    @pl.when(i < n_prompt_blocks)
    def _():
        pos = i * rows + lax.broadcasted_iota(jnp.int32, (rows, 1), 0)
        for g, win in enumerate(POOL_WINDOWS):
            cur = cur_refs[g][...]
            halo = jnp.where(i > 0, halo_refs[g][...], 0.0)
            a = jnp.concatenate([halo, cur], axis=0)
            sh = 1
            while sh < win:
                a = a + pltpu.roll(a, sh, 0)
                sh *= 2
            cnt = jnp.minimum(pos + 1, win).astype(F32)
            d = a[POOL_HALO:, :] / cnt - cur
            sl = slice(g * pg, (g + 1) * pg)
            y = jnp.dot(d.astype(BF16), w_ref[g].astype(BF16), preferred_element_type=F32)
            o_ref[:, sl] = (y * s_ref[:, sl]).astype(o_ref.dtype)

    @pl.when(i == n_prompt_blocks)
    def _():
        ctx = st_ref.shape[1]
        for g, win in enumerate(POOL_WINDOWS):
            sl = slice(g * pg, (g + 1) * pg)
            cur = cur_refs[g][0:n_seq, :]
            acc = cur
            for jj in range(1, win):
                acc = acc + st_ref[:, ctx - jj, sl]
            cnt = float(min(PAST_LEN + 1, win))
            d = acc / cnt - cur
            y = jnp.dot(d.astype(BF16), w_ref[g].astype(BF16), preferred_element_type=F32)
            o_ref[0:n_seq, sl] = (y * s_ref[:, sl]).astype(o_ref.dtype)


def _pool_branch(z, state, w_pool, pool_scale, n_tok, n_seq, u_col0):
    m = z.shape[0]
    ng, pg, _ = w_pool.shape
    pw = ng * pg
    rows = 256
    assert n_seq <= rows and m == n_tok + n_seq
    c0 = u_col0 // pg
    scale = pool_scale.reshape(1, pw)
    hpb = rows // POOL_HALO
    npb = n_tok // rows

    cur_specs = [pl.BlockSpec((rows, pg), lambda i, g=g: (i, c0 + g)) for g in range(ng)]
    halo_specs = [pl.BlockSpec((POOL_HALO, pg),
                               lambda i, g=g: (jnp.maximum(i * hpb - 1, 0), c0 + g))
                  for g in range(ng)]
    return pl.pallas_call(
        functools.partial(_pool_kernel, rows=rows, n_prompt_blocks=npb, n_seq=n_seq),
        grid=(npb + 1,),
        in_specs=cur_specs + halo_specs + [
            pl.BlockSpec(state.shape, lambda i: (0, 0, 0)),
            pl.BlockSpec((ng, pg, pg), lambda i: (0, 0, 0)),
            pl.BlockSpec((1, pw), lambda i: (0, 0))],
        out_specs=pl.BlockSpec((rows, pw), lambda i: (i, 0)),
        out_shape=jax.ShapeDtypeStruct((m, pw), BF16),
        compiler_params=_params(("parallel",)),
        name="pool",
    )(*([z] * (2 * ng)), state, w_pool, scale)


def _mix_kernel(oa_ref, po_ref, wa_ref, wp_ref, ga_ref, gb_ref, o_ref, wab_ref, wpb_ref):
    @pl.when(pl.program_id(1) == 0)
    def _():
        wab_ref[...] = wa_ref[...].astype(BF16)
        wpb_ref[...] = wp_ref[...].astype(BF16)

    a = jnp.dot(oa_ref[...], wab_ref[...], preferred_element_type=F32)
    p = jnp.dot(po_ref[...], wpb_ref[...], preferred_element_type=F32)
    o_ref[...] = (ga_ref[...] * a + gb_ref[...] * p).astype(o_ref.dtype)


def _mix(o_attn, pool_out, w_a, w_p, z, gate_col0):
    m, ka = o_attn.shape
    kp = pool_out.shape[1]
    n = w_a.shape[1]
    ga0 = gate_col0 // COL_TILE
    gb0 = (gate_col0 + n) // COL_TILE
    return pl.pallas_call(
        _mix_kernel,
        grid=(n // COL_TILE, m // ROW_TILE),
        in_specs=[pl.BlockSpec((ROW_TILE, ka), lambda j, i: (i, 0)),
                  pl.BlockSpec((ROW_TILE, kp), lambda j, i: (i, 0)),
                  pl.BlockSpec((ka, COL_TILE), lambda j, i: (0, j)),
                  pl.BlockSpec((kp, COL_TILE), lambda j, i: (0, j)),
                  pl.BlockSpec((ROW_TILE, COL_TILE), lambda j, i: (i, ga0 + j)),
                  pl.BlockSpec((ROW_TILE, COL_TILE), lambda j, i: (i, gb0 + j))],
        out_specs=pl.BlockSpec((ROW_TILE, COL_TILE), lambda j, i: (i, j)),
        out_shape=jax.ShapeDtypeStruct((m, n), BF16),
        scratch_shapes=[pltpu.VMEM((ka, COL_TILE), BF16), pltpu.VMEM((kp, COL_TILE), BF16)],
        compiler_params=_params(("arbitrary", "arbitrary")),
        name="mix",
    )(o_attn, pool_out, w_a, w_p, z, z)


def _outproj_kernel(a_ref, w_ref, xp_ref, xs_ref, o_ref, wb_ref, *, split):
    i = pl.program_id(1)
    last = pl.num_programs(1) - 1

    @pl.when(i == 0)
    def _():
        wb_ref[...] = w_ref[...].astype(BF16)

    y = jnp.dot(a_ref[...], wb_ref[...], preferred_element_type=F32)

    @pl.when(i < last)
    def _():
        o_ref[...] = xp_ref[...] + y

    @pl.when(i == last)
    def _():
        o_ref[0:split, :] = xp_ref[0:split, :] + y[0:split, :]
        o_ref[split:, :] = xs_ref[...] + y[split:, :]


def _outproj(mixed, w_out, xp, xs):
    m, k = mixed.shape
    n = w_out.shape[1]
    n_tok, n_seq = xp.shape[0], xs.shape[0]
    nblk = m // ROW_TILE
    split = n_tok - (nblk - 1) * ROW_TILE
    assert 0 < split and split + n_seq == ROW_TILE and split % 8 == 0
    return pl.pallas_call(
        functools.partial(_outproj_kernel, split=split),
        grid=(n // COL_TILE, nblk),
        in_specs=[pl.BlockSpec((ROW_TILE, k), lambda j, i: (i, 0)),
                  pl.BlockSpec((k, COL_TILE), lambda j, i: (0, j)),
                  pl.BlockSpec((ROW_TILE, COL_TILE), lambda j, i: (i, j)),
                  pl.BlockSpec((n_seq, COL_TILE), lambda j, i: (0, j))],
        out_specs=pl.BlockSpec((ROW_TILE, COL_TILE), lambda j, i: (i, j)),
        out_shape=jax.ShapeDtypeStruct((m, n), F32),
        scratch_shapes=[pltpu.VMEM((k, COL_TILE), BF16)],
        compiler_params=_params(("arbitrary", "arbitrary")),
        name="outproj",
    )(mixed, w_out, xp, xs)


def _router_kernel(x_ref, g_ref, wr_ref, br_ref, h_ref, e_ref, r_ref, w_ref, c_ref, carry_ref):
    i = pl.program_id(0)

    @pl.when(i == 0)
    def _():
        carry_ref[...] = jnp.zeros_like(carry_ref)

    x = x_ref[...]
    ms = jnp.mean(x * x, axis=-1, keepdims=True)
    h = x * lax.rsqrt(ms + RMS_EPS) * g_ref[...]
    h_ref[...] = h
    logits = jnp.dot(h.astype(BF16), wr_ref[...].astype(BF16),
                     preferred_element_type=F32) + br_ref[...]
    tb, ne = logits.shape
    elane = lax.broadcasted_iota(jnp.int32, (tb, ne), 1).astype(F32)
    work = logits
    sel = jnp.zeros((tb, ne), F32)
    idxs, vals = [], []
    for _ in range(TOP_K):
        mx = jnp.max(work, axis=-1, keepdims=True)
        idx = jnp.min(jnp.where(work == mx, elane, float(ne)), axis=-1, keepdims=True)
        hit = elane == idx
        sel = jnp.where(hit, 1.0, sel)
        work = jnp.where(hit, -jnp.inf, work)
        idxs.append(idx)
        vals.append(mx)
    exps = [jnp.exp(v - vals[0]) for v in vals]
    den = functools.reduce(lambda a, b: a + b, exps)

    ri = lax.broadcasted_iota(jnp.int32, (tb, tb), 0)
    ci = lax.broadcasted_iota(jnp.int32, (tb, tb), 1)
    tril = jnp.where(ci < ri, 1.0, 0.0).astype(BF16)
    before = jnp.dot(tril, sel.astype(BF16), preferred_element_type=F32) + carry_ref[0:1, :]
    carry_ref[...] = carry_ref[...] + jnp.sum(sel, axis=0, keepdims=True)

    lane = lax.broadcasted_iota(jnp.int32, (tb, HEAD_DIM), 1)
    e_out = jnp.zeros((tb, HEAD_DIM), jnp.int32)
    r_out = jnp.zeros((tb, HEAD_DIM), jnp.int32)
    w_out = jnp.zeros((tb, HEAD_DIM), F32)
    for k in range(TOP_K):
        rank = jnp.sum(jnp.where(elane == idxs[k], before, 0.0), axis=-1, keepdims=True)
        e_out = jnp.where(lane == k, idxs[k].astype(jnp.int32), e_out)
        r_out = jnp.where(lane == k, rank.astype(jnp.int32), r_out)
        w_out = jnp.where(lane == k, exps[k] / den, w_out)
    e_ref[...] = e_out
    r_ref[...] = r_out
    w_ref[...] = w_out
    c_ref[...] = carry_ref[...].astype(jnp.int32)


def _router(x1, gain, w_router, b_router):
    m, d = x1.shape
    ne = w_router.shape[1]
    nblk = m // TOK_BLOCK
    lanes = HEAD_DIM
    return pl.pallas_call(
        _router_kernel,
        grid=(nblk,),
        in_specs=[pl.BlockSpec((TOK_BLOCK, d), lambda i: (i, 0)),
                  pl.BlockSpec((1, d), lambda i: (0, 0)),
                  pl.BlockSpec((d, ne), lambda i: (0, 0)),
                  pl.BlockSpec((1, ne), lambda i: (0, 0))],
        out_specs=[pl.BlockSpec((TOK_BLOCK, d), lambda i: (i, 0)),
                   pl.BlockSpec((TOK_BLOCK, lanes), lambda i: (i, 0)),
                   pl.BlockSpec((TOK_BLOCK, lanes), lambda i: (i, 0)),
                   pl.BlockSpec((TOK_BLOCK, lanes), lambda i: (i, 0)),
                   pl.BlockSpec((8, ne), lambda i: (0, 0))],
        out_shape=[jax.ShapeDtypeStruct((m, d), F32),
                   jax.ShapeDtypeStruct((m, lanes), jnp.int32),
                   jax.ShapeDtypeStruct((m, lanes), jnp.int32),
                   jax.ShapeDtypeStruct((m, lanes), F32),
                   jax.ShapeDtypeStruct((8, ne), jnp.int32)],
        scratch_shapes=[pltpu.VMEM((8, ne), F32)],
        compiler_params=_params(("arbitrary",)),
        name="router",
    )(x1, gain.reshape(1, d), w_router, b_router.reshape(1, ne))


def _dispatch_kernel(tok_ref, h_hbm, o_ref, buf_ref, sem):
    rows = buf_ref.shape[0]

    def row_copy(r):
        return pltpu.make_async_copy(h_hbm.at[pl.ds(tok_ref[0, 0, r], 1), :],
                                     buf_ref.at[pl.ds(r, 1), :], sem)

    def start(r2, c):
        row_copy(2 * r2).start(priority=0)
        row_copy(2 * r2 + 1).start(priority=1)
        return c

    def wait(r, c):
        row_copy(r).wait()
        return c

    lax.fori_loop(0, rows // 2, start, 0)
    lax.fori_loop(0, rows, wait, 0)
    o_ref[...] = buf_ref[...].astype(o_ref.dtype)


def _dispatch(h, row_tok, n_rows):
    d = h.shape[1]
    nu = n_rows // EXPERT_TILE
    return pl.pallas_call(
        _dispatch_kernel,
        grid=(nu,),
        in_specs=[pl.BlockSpec((1, 1, EXPERT_TILE), lambda u: (u, 0, 0),
                               memory_space=pltpu.SMEM),
                  pl.BlockSpec(memory_space=pl.ANY)],
        out_specs=pl.BlockSpec((EXPERT_TILE, d), lambda u: (u, 0)),
        out_shape=jax.ShapeDtypeStruct((n_rows, d), BF16),
        scratch_shapes=[pltpu.VMEM((EXPERT_TILE, d), F32), pltpu.SemaphoreType.DMA(())],
        compiler_params=_params(("arbitrary",)),
        name="dispatch",
    )(row_tok.reshape(nu, 1, EXPERT_TILE), h)


def _stream_weight_tiles(w_hbm, col_offsets, land, wbf_refs, sem,
                         ue_ref, uf_ref, nxe_ref, nxw_ref):
    f = pl.program_id(0)
    u = pl.program_id(1)
    nf = pl.num_programs(0)
    parts = range(len(col_offsets))

    def tile_copy(e, fcol, part):
        col = pl.multiple_of(fcol * EXPERT_COLS + col_offsets[part], EXPERT_COLS)
        return pltpu.make_async_copy(w_hbm.at[0, e, :, pl.ds(col, EXPERT_COLS)],
                                     land.at[part], sem.at[part])

    @pl.when(jnp.logical_and(f == 0, u == 0))
    def _():
        for part in parts:
            tile_copy(ue_ref[0], 0, part).start()

    def cast_rows(part):
        def body(i, carry):
            r = pl.multiple_of(i * CAST_ROWS, CAST_ROWS)
            wbf_refs[part][pl.ds(r, CAST_ROWS), :] = (
                land[part, pl.ds(r, CAST_ROWS), :].astype(BF16))
            return carry
        lax.fori_loop(0, land.shape[1] // CAST_ROWS, body, 0)

    @pl.when(uf_ref[u] == 1)
    def _():
        for part in parts:
            tile_copy(ue_ref[u], f, part).wait()
            cast_rows(part)
        wraps = nxw_ref[u]

        @pl.when(jnp.logical_not(jnp.logical_and(wraps == 1, f == nf - 1)))
        def _():
            for part in parts:
                tile_copy(nxe_ref[u], f + wraps, part).start()


def _gate_up_kernel(ue_ref, ub_ref, uf_ref, nxe_ref, nxw_ref, na_ref, x_ref, w_hbm, bg_ref,
                    bu_ref, o_ref, land, wgb_ref, wub_ref, sem):
    del ub_ref
    u = pl.program_id(1)
    ff = pl.num_programs(0) * EXPERT_COLS
    _stream_weight_tiles(w_hbm, (0, ff), land, (wgb_ref, wub_ref), sem,
                         ue_ref, uf_ref, nxe_ref, nxw_ref)

    @pl.when(u < na_ref[0])
    def _():
        for c in range(EXPERT_COLS // COL_TILE):
            sl = slice(c * COL_TILE, (c + 1) * COL_TILE)
            gate = jnp.dot(x_ref[...], wgb_ref[:, sl],
                           preferred_element_type=F32) + bg_ref[:, sl]
            up = jnp.dot(x_ref[...], wub_ref[:, sl],
                         preferred_element_type=F32) + bu_ref[:, sl]
            gate = jnp.minimum(gate, SWIGLU_LIMIT)
            up = jnp.clip(up, -SWIGLU_LIMIT, SWIGLU_LIMIT)
            act = gate * jax.nn.sigmoid(SWIGLU_ALPHA * gate) * (up + 1.0)
            o_ref[:, sl] = act.astype(o_ref.dtype)

    @pl.when(u >= na_ref[0])
    def _():
        o_ref[...] = jnp.zeros_like(o_ref)


def _unit_imap(fn):
    return lambda c, u, ue, ub, uf, nxe, nxw, na: fn(c, u, ue, ub)


def _gate_up(xs, w_gate_up, b_gate_up, units):
    n_rows, d = xs.shape
    ne = w_gate_up.shape[1]
    ff = w_gate_up.shape[3] // 2
    nu = n_rows // EXPERT_TILE
    nf = ff // EXPERT_COLS
    bias = b_gate_up.reshape(ne, 1, 2 * ff)
    grid_spec = pltpu.PrefetchScalarGridSpec(
        num_scalar_prefetch=len(units),
        grid=(nf, nu),
        in_specs=[
            pl.BlockSpec((EXPERT_TILE, d), _unit_imap(lambda f, u, ue, ub: (ub[u], 0))),
            pl.BlockSpec(memory_space=pl.ANY),
            pl.BlockSpec((None, 1, EXPERT_COLS),
                         _unit_imap(lambda f, u, ue, ub: (ue[u], 0, f))),
            pl.BlockSpec((None, 1, EXPERT_COLS),
                         _unit_imap(lambda f, u, ue, ub: (ue[u], 0, nf + f))),
        ],
        out_specs=pl.BlockSpec((EXPERT_TILE, EXPERT_COLS),
                               _unit_imap(lambda f, u, ue, ub: (u, f))),
        scratch_shapes=[pltpu.VMEM((2, d, EXPERT_COLS), F32),
                        pltpu.VMEM((d, EXPERT_COLS), BF16),
                        pltpu.VMEM((d, EXPERT_COLS), BF16),
                        pltpu.SemaphoreType.DMA((2,))],
    )
    return pl.pallas_call(
        _gate_up_kernel,
        grid_spec=grid_spec,
        out_shape=jax.ShapeDtypeStruct((n_rows, ff), BF16),
        compiler_params=_params(("arbitrary", "arbitrary"), EXPERT_VMEM_LIMIT),
        name="expert_gate_up",
    )(*units, xs, w_gate_up, bias, bias)


def _down_kernel(ue_ref, ub_ref, uf_ref, nxe_ref, nxw_ref, na_ref, a_ref, w_hbm, b_ref,
                 o_ref, land, wb_ref, sem):
    del ub_ref
    u = pl.program_id(1)
    _stream_weight_tiles(w_hbm, (0,), land, (wb_ref,), sem,
                         ue_ref, uf_ref, nxe_ref, nxw_ref)

    @pl.when(u < na_ref[0])
    def _():
        o_ref[...] = jnp.dot(a_ref[...], wb_ref[...], preferred_element_type=F32) + b_ref[...]

    @pl.when(u >= na_ref[0])
    def _():
        o_ref[...] = jnp.zeros_like(o_ref)


def _down(act, w_down, b_down, units):
    n_rows, ff = act.shape
    ne = w_down.shape[1]
    d = w_down.shape[3]
    nu = n_rows // EXPERT_TILE
    bias = b_down.reshape(ne, 1, d)
    grid_spec = pltpu.PrefetchScalarGridSpec(
        num_scalar_prefetch=len(units),
        grid=(d // EXPERT_COLS, nu),
        in_specs=[
            pl.BlockSpec((EXPERT_TILE, ff), _unit_imap(lambda n, u, ue, ub: (ub[u], 0))),
            pl.BlockSpec(memory_space=pl.ANY),
            pl.BlockSpec((None, 1, EXPERT_COLS),
                         _unit_imap(lambda n, u, ue, ub: (ue[u], 0, n))),
        ],
        out_specs=pl.BlockSpec((EXPERT_TILE, EXPERT_COLS),
                               _unit_imap(lambda n, u, ue, ub: (u, n))),
        scratch_shapes=[pltpu.VMEM((1, ff, EXPERT_COLS), F32),
                        pltpu.VMEM((ff, EXPERT_COLS), BF16),
                        pltpu.SemaphoreType.DMA((1,))],
    )
    return pl.pallas_call(
        _down_kernel,
        grid_spec=grid_spec,
        out_shape=jax.ShapeDtypeStruct((n_rows, d), F32),
        compiler_params=_params(("arbitrary", "arbitrary")),
        name="expert_down",
    )(*units, act, w_down, bias)


def _combine_kernel(pos_ref, x_ref, w_ref, y_hbm, op_ref, os_ref, buf_ref, sem, *,
                    n_prompt_blocks):
    tb = x_ref.shape[0]
    i = pl.program_id(0)

    def row_copy(t, k):
        return pltpu.make_async_copy(y_hbm.at[pl.ds(pos_ref[0, 0, t * TOP_K + k], 1), :],
                                     buf_ref.at[k, pl.ds(t, 1), :], sem)

    def start(t, c):
        for k in range(TOP_K):
            row_copy(t, k).start(priority=k % 2)
        return c

    def wait(t, c):
        for k in range(TOP_K):
            row_copy(t, k).wait()
        return c

    lax.fori_loop(0, tb, start, 0)
    lax.fori_loop(0, tb, wait, 0)
    w = w_ref[...]
    acc = w[:, 0:1] * buf_ref[0]
    for k in range(1, TOP_K):
        acc = acc + w[:, k:k + 1] * buf_ref[k]
    y = x_ref[...] + acc

    @pl.when(i < n_prompt_blocks)
    def _():
        op_ref[...] = y

    @pl.when(i >= n_prompt_blocks)
    def _():
        os_ref[...] = y


def _combine(x1, wts, pos, y_sorted, n_tok, n_seq):
    m, d = x1.shape
    nblk = m // TOK_BLOCK
    npb = n_tok // TOK_BLOCK
    assert n_seq == TOK_BLOCK and npb + 1 == nblk
    return pl.pallas_call(
        functools.partial(_combine_kernel, n_prompt_blocks=npb),
        grid=(nblk,),
        in_specs=[pl.BlockSpec((1, 1, TOK_BLOCK * TOP_K), lambda i: (i, 0, 0),
                               memory_space=pltpu.SMEM),
                  pl.BlockSpec((TOK_BLOCK, d), lambda i: (i, 0)),
                  pl.BlockSpec((TOK_BLOCK, wts.shape[1]), lambda i: (i, 0)),
                  pl.BlockSpec(memory_space=pl.ANY)],
        out_specs=[pl.BlockSpec((TOK_BLOCK, d), lambda i: (jnp.minimum(i, npb - 1), 0)),
                   pl.BlockSpec((TOK_BLOCK, d), lambda i: (0, 0))],
        out_shape=[jax.ShapeDtypeStruct((n_tok, d), F32),
                   jax.ShapeDtypeStruct((n_seq, d), F32)],
        scratch_shapes=[pltpu.VMEM((TOP_K, TOK_BLOCK, d), F32), pltpu.SemaphoreType.DMA(())],
        compiler_params=_params(("arbitrary",)),
        name="combine",
    )(pos.reshape(nblk, 1, TOK_BLOCK * TOP_K), x1, wts, y_sorted)


def _moe(x1, layer_w, n_tok):
    norm_ffn, w_router, b_router, w_gate_up, b_gate_up, w_down, b_down = layer_w
    m = x1.shape[0]
    ne = w_router.shape[1]
    h, eidx, rank, wts, counts = _router(x1, norm_ffn, w_router, b_router)

    tile = EXPERT_TILE
    counts = counts[0]
    padded = (counts + tile - 1) // tile * tile
    pend = jnp.cumsum(padded)
    pstart = pend - padded
    eidx = eidx[:, :TOP_K]
    pos = (pstart[eidx] + rank[:, :TOP_K]).astype(jnp.int32)
    n_rows = (m * TOP_K + ne * (tile - 1) + tile - 1) // tile * tile
    nu = n_rows // tile
    tok = jnp.broadcast_to(jnp.arange(m, dtype=jnp.int32)[:, None], pos.shape)
    row_tok = jnp.zeros((n_rows,), jnp.int32).at[pos.reshape(-1)].set(tok.reshape(-1))
    n_active = (pend[-1] // tile).astype(jnp.int32)
    u = jnp.arange(nu, dtype=jnp.int32)
    unit_blk = jnp.minimum(u, n_active - 1)
    unit_e = jnp.sum((pend[None, :] <= (unit_blk * tile)[:, None]).astype(jnp.int32), axis=1)
    unit_e = jnp.minimum(unit_e, ne - 1)
    prev_e = jnp.concatenate([jnp.full((1,), -1, jnp.int32), unit_e[:-1]])
    first = (u < n_active) & (unit_e != prev_e)
    unit_first = first.astype(jnp.int32)
    later = first[None, :] & (u[None, :] > u[:, None])
    nxt = jnp.min(jnp.where(later, u[None, :], nu), axis=1)
    next_wraps = (nxt >= nu).astype(jnp.int32)
    next_e = jnp.where(nxt >= nu, unit_e[0], unit_e[jnp.minimum(nxt, nu - 1)])
    units = (unit_e, unit_blk, unit_first, next_e.astype(jnp.int32), next_wraps,
             n_active.reshape(1))

    xs = _dispatch(h, row_tok, n_rows)
    act = _gate_up(xs, w_gate_up, b_gate_up, units)
    ys = _down(act, w_down, b_down, units)
    return _combine(x1, wts, pos, ys, n_tok, m - n_tok)


def kernel(x_prompt, x_sample, cache_kv_w128, cache_kv_w512, cache_kv_w2048, state_pool,
           norm_mix, w_in, q_norm, k_norm, w_pool_group, pool_scale, w_branch_attn,
           w_branch_pool, w_out, norm_ffn, w_router, b_router, w_gate_up, b_gate_up,
           w_down, b_down):
    depth = norm_mix.shape[0]
    assert depth == 1 and x_prompt.shape[0] == 1 and x_sample.shape[1] == 1
    n_tok, d_model = x_prompt.shape[1], x_prompt.shape[2]
    n_seq = x_sample.shape[0]
    caches = (cache_kv_w128, cache_kv_w512, cache_kv_w2048)
    n_grp = len(ATTN_GROUPS)
    hw = HEADS * HEAD_DIM
    qkv_w = n_grp * hw
    pool_w = w_pool_group.shape[1] * w_pool_group.shape[2]
    width = w_in.shape[2]
    layer = 0

    xp, xs = x_prompt[0], x_sample[:, 0]

    ones = jnp.ones((width - 2 * qkv_w,), F32)
    colgain = jnp.concatenate([
        jnp.broadcast_to(q_norm[layer][:, None, :], (n_grp, HEADS, HEAD_DIM)).reshape(-1),
        jnp.broadcast_to(k_norm[layer][:, None, :], (n_grp, HEADS, HEAD_DIM)).reshape(-1),
        ones]).reshape(1, width)

    h = _rmsnorm(xp, xs, norm_mix[layer])
    z = _inproj(h, w_in[layer], colgain, 2 * qkv_w, qkv_w + pool_w)

    po, plse, so, slse = [], [], [], []
    for g, (window, dil) in enumerate(ATTN_GROUPS):
        assert window == BAND * dil
        o, lse = _attn_prompt(z, g, dil, n_tok)
        po.append(o)
        plse.append(lse)
        o, lse = _attn_sample(z, caches[g][layer], g, dil, n_tok, n_seq)
        so.append(o)
        slse.append(lse)
    o_attn = _merge_groups(po, plse, so, slse)

    pool_out = _pool_branch(z, state_pool[layer], w_pool_group[layer], pool_scale[layer],
                            n_tok, n_seq, 3 * qkv_w)
    mixed = _mix(o_attn, pool_out, w_branch_attn[layer], w_branch_pool[layer], z,
                 3 * qkv_w + pool_w)
    x1 = _outproj(mixed, w_out[layer], xp, xs)
    y_p, y_s = _moe(x1, (norm_ffn[layer], w_router[layer], b_router[layer],
                         w_gate_up, b_gate_up, w_down, b_down), n_tok)
    y_prompt = y_p[None]
    y_sample = y_s[:, None]

    outs = [y_prompt, y_sample]
    k_all = z[:, qkv_w:2 * qkv_w]
    v_all = z[:, 2 * qkv_w:3 * qkv_w]
    for g, (window, dil) in enumerate(ATTN_GROUPS):
        keep = min(window, n_tok)
        sl = slice(g * hw, (g + 1) * hw)
        kp = k_all[n_tok - keep:n_tok, sl].reshape(keep, HEADS, HEAD_DIM)
        vp = v_all[n_tok - keep:n_tok, sl].reshape(keep, HEADS, HEAD_DIM)
        outs.append(jnp.stack([kp, vp], axis=0)[None, None])
        ks = k_all[n_tok:, sl].reshape(n_seq, 1, HEADS, HEAD_DIM)
        vs = v_all[n_tok:, sl].reshape(n_seq, 1, HEADS, HEAD_DIM)
        outs.append(jnp.stack([ks, vs], axis=1)[None])
    u_all = z[:, 3 * qkv_w:3 * qkv_w + pool_w]
    keep = min(POOL_HALO - 1, n_tok)
    outs.append(u_all[n_tok - keep:n_tok][None, None])
    outs.append(u_all[n_tok:][:, None][None])
    return tuple(outs)
```

```python
import functools

import jax
import jax.numpy as jnp
from jax import lax
from jax.experimental import pallas as pl
from jax.experimental.pallas import tpu as pltpu

F32 = jnp.float32
BF16 = jnp.bfloat16

HEAD_DIM = 128
HEADS = 8
ATTN_GROUPS = ((128, 1), (512, 4), (2048, 16))
BAND = 128
ATTN_ROWS = 2048
POOL_WINDOWS = (2, 4, 8, 16)
POOL_HALO = 16
TOP_K = 4
SWIGLU_LIMIT = 7.0
SWIGLU_ALPHA = 1.702
RMS_EPS = 1e-5
NEG_INF = -1e30
PAST_LEN = 2048

ROW_TILE = 640
COL_TILE = 512
TOK_BLOCK = 128
EXPERT_TILE = 256
EXPERT_COLS = 1024
CAST_ROWS = 256
VMEM_LIMIT = 56 * 1024 * 1024
EXPERT_VMEM_LIMIT = 62 * 1024 * 1024


def _params(semantics, vmem=VMEM_LIMIT):
    return pltpu.CompilerParams(dimension_semantics=semantics, vmem_limit_bytes=vmem)


def _rmsnorm_kernel(xp_ref, xs_ref, g_ref, o_ref, *, n_prompt_blocks):
    i = pl.program_id(0)

    def norm(x_ref):
        x = x_ref[...]
        ms = jnp.mean(x * x, axis=-1, keepdims=True)
        o_ref[...] = (x * lax.rsqrt(ms + RMS_EPS) * g_ref[...]).astype(o_ref.dtype)

    @pl.when(i < n_prompt_blocks)
    def _():
        norm(xp_ref)

    @pl.when(i >= n_prompt_blocks)
    def _():
        norm(xs_ref)


def _rmsnorm(xp, xs, gain):
    n_tok, d = xp.shape
    n_seq = xs.shape[0]
    npb, nsb = n_tok // TOK_BLOCK, n_seq // TOK_BLOCK
    return pl.pallas_call(
        functools.partial(_rmsnorm_kernel, n_prompt_blocks=npb),
        grid=(npb + nsb,),
        in_specs=[pl.BlockSpec((TOK_BLOCK, d), lambda i: (jnp.minimum(i, npb - 1), 0)),
                  pl.BlockSpec((TOK_BLOCK, d), lambda i: (jnp.maximum(i - npb, 0), 0)),
                  pl.BlockSpec((1, d), lambda i: (0, 0))],
        out_specs=pl.BlockSpec((TOK_BLOCK, d), lambda i: (i, 0)),
        out_shape=jax.ShapeDtypeStruct((n_tok + n_seq, d), BF16),
        compiler_params=_params(("parallel",)),
        name="rmsnorm",
    )(xp, xs, gain.reshape(1, d))


def _inproj_kernel(h_ref, w_ref, cg_ref, o_ref, wb_ref, *, qk_tiles, plain_end):
    j = pl.program_id(0)
    i = pl.program_id(1)

    @pl.when(i == 0)
    def _():
        wb_ref[...] = w_ref[...].astype(BF16)

    acc = jnp.dot(h_ref[...], wb_ref[...], preferred_element_type=F32)

    @pl.when(j < qk_tiles)
    def _():
        for hh in range(acc.shape[1] // HEAD_DIM):
            sl = slice(hh * HEAD_DIM, (hh + 1) * HEAD_DIM)
            blk = acc[:, sl]
            ms = jnp.mean(blk * blk, axis=-1, keepdims=True)
            o_ref[:, sl] = blk * lax.rsqrt(ms + RMS_EPS) * cg_ref[:, sl]

    @pl.when(jnp.logical_and(j >= qk_tiles, j < plain_end))
    def _():
        o_ref[...] = acc

    @pl.when(j >= plain_end)
    def _():
        o_ref[...] = jax.nn.sigmoid(acc)


def _inproj(h, w_in, colgain, qk_width, plain_width):
    m, d = h.shape
    n = w_in.shape[1]
    kern = functools.partial(_inproj_kernel, qk_tiles=qk_width // COL_TILE,
                             plain_end=(qk_width + plain_width) // COL_TILE)
    return pl.pallas_call(
        kern,
        grid=(n // COL_TILE, m // ROW_TILE),
        in_specs=[pl.BlockSpec((ROW_TILE, d), lambda j, i: (i, 0)),
                  pl.BlockSpec((d, COL_TILE), lambda j, i: (0, j)),
                  pl.BlockSpec((1, COL_TILE), lambda j, i: (0, j))],
        out_specs=pl.BlockSpec((ROW_TILE, COL_TILE), lambda j, i: (i, j)),
        out_shape=jax.ShapeDtypeStruct((m, n), F32),
        scratch_shapes=[pltpu.VMEM((d, COL_TILE), BF16)],
        compiler_params=_params(("arbitrary", "arbitrary")),
        name="inproj",
    )(h, w_in, colgain)


def _attn_prompt_kernel(q_ref, kh_ref, k_ref, vh_ref, v_ref, o_ref, lse_ref, *, dil):
    n = pl.program_id(1)
    span = BAND * dil
    qi = lax.broadcasted_iota(jnp.int32, (BAND, 2 * BAND), 0) + BAND
    ki = lax.broadcasted_iota(jnp.int32, (BAND, 2 * BAND), 1)
    dist = qi - ki
    band = (dist >= 0) & (dist <= BAND)
    first_key = jnp.where(n > 0, 0, BAND)
    band_first = band & (ki >= first_key)
    scale = HEAD_DIM ** -0.5

    def rows(ref, start, size):
        if dil == 1:
            return ref[pl.ds(start, size), :]
        return ref[pl.ds(start, size, stride=dil), :]

    for c in range(ATTN_ROWS // span):
        for r in range(dil):
            base = c * span + r
            q = rows(q_ref, base, BAND).astype(BF16)
            if c == 0:
                kc = jnp.concatenate([rows(kh_ref, r, BAND), rows(k_ref, r, BAND)], axis=0)
                vc = jnp.concatenate([rows(vh_ref, r, BAND), rows(v_ref, r, BAND)], axis=0)
                valid = band_first
            else:
                kc = rows(k_ref, base - span, 2 * BAND)
                vc = rows(v_ref, base - span, 2 * BAND)
                valid = band
            s = lax.dot_general(q, kc.astype(BF16), (((1,), (1,)), ((), ())),
                                preferred_element_type=F32) * scale
            s = jnp.where(valid, s, NEG_INF)
            m = jnp.max(s, axis=-1, keepdims=True)
            p = jnp.exp(s - m)
            l = jnp.sum(p, axis=-1, keepdims=True)
            o = jnp.dot(p.astype(BF16), vc.astype(BF16), preferred_element_type=F32) / l
            lse = jnp.broadcast_to(m + jnp.log(l), (BAND, HEAD_DIM))
            if dil == 1:
                o_ref[pl.ds(base, BAND), :] = o
                lse_ref[pl.ds(base, BAND), :] = lse
            else:
                o_ref[pl.ds(base, BAND, stride=dil), :] = o
                lse_ref[pl.ds(base, BAND, stride=dil), :] = lse


def _attn_prompt(z, g, dil, n_tok):
    hw = HEADS * HEAD_DIM
    n_grp = len(ATTN_GROUPS)
    span = BAND * dil
    spb = ATTN_ROWS // span
    qcol, kcol, vcol = (g * HEADS, (n_grp + g) * HEADS, (2 * n_grp + g) * HEADS)

    def cur(col):
        return pl.BlockSpec((ATTN_ROWS, HEAD_DIM), lambda h, n: (n, col + h))

    def halo(col):
        return pl.BlockSpec((span, HEAD_DIM),
                            lambda h, n: (jnp.maximum(n * spb - 1, 0), col + h))

    out_spec = pl.BlockSpec((ATTN_ROWS, HEAD_DIM), lambda h, n: (n, h))
    return pl.pallas_call(
        functools.partial(_attn_prompt_kernel, dil=dil),
        grid=(HEADS, n_tok // ATTN_ROWS),
        in_specs=[cur(qcol), halo(kcol), cur(kcol), halo(vcol), cur(vcol)],
        out_specs=[out_spec, out_spec],
        out_shape=[jax.ShapeDtypeStruct((n_tok, hw), F32)] * 2,
        compiler_params=_params(("parallel", "parallel")),
        name=f"attn_prompt_d{dil}",
    )(z, z, z, z, z)


def _attn_sample_kernel(q_ref, kn_ref, vn_ref, kv_ref, o_ref, lse_ref, *, rows):
    scale = HEAD_DIM ** -0.5

    def to_heads(row):
        return jnp.concatenate(
            [row[:, h * HEAD_DIM:(h + 1) * HEAD_DIM] for h in range(HEADS)], axis=0)

    def to_row(x):
        return jnp.concatenate([x[h:h + 1, :] for h in range(HEADS)], axis=1)

    o_rows, lse_rows = [], []
    for b in range(rows):
        q = to_heads(q_ref[b:b + 1, :])
        kn = to_heads(kn_ref[b:b + 1, :])
        vn = to_heads(vn_ref[b:b + 1, :])
        k = kv_ref[b, 0]
        v = kv_ref[b, 1]
        s = jnp.sum(k * q[None], axis=-1, keepdims=True) * scale
        sn = jnp.sum(q * kn, axis=-1, keepdims=True) * scale
        m = jnp.maximum(jnp.max(s, axis=0), sn)
        p = jnp.exp(s - m[None])
        pn = jnp.exp(sn - m)
        l = jnp.sum(p, axis=0) + pn
        o = (jnp.sum(p * v, axis=0) + pn * vn) / l
        lse = jnp.broadcast_to(m + jnp.log(l), (HEADS, HEAD_DIM))
        o_rows.append(to_row(o))
        lse_rows.append(to_row(lse))
    o_ref[...] = jnp.concatenate(o_rows, axis=0)
    lse_ref[...] = jnp.concatenate(lse_rows, axis=0)


def _attn_sample(z, cache, g, dil, row0, n_seq):
    hw = HEADS * HEAD_DIM
    n_grp = len(ATTN_GROUPS)
    length = cache.shape[2]
    assert length == BAND * dil, "cache must hold exactly one window"
    kv = cache.reshape(n_seq, 2, BAND, dil, HEADS, HEAD_DIM)
    rows = 8
    blk0 = row0 // rows

    def zspec(col):
        return pl.BlockSpec((rows, hw), lambda i: (blk0 + i, col))

    out_spec = pl.BlockSpec((rows, hw), lambda i: (i, 0))
    return pl.pallas_call(
        functools.partial(_attn_sample_kernel, rows=rows),
        grid=(n_seq // rows,),
        in_specs=[zspec(g), zspec(n_grp + g), zspec(2 * n_grp + g),
                  pl.BlockSpec((rows, 2, BAND, None, HEADS, HEAD_DIM),
                               lambda i: (i, 0, 0, 0, 0, 0))],
        out_specs=[out_spec, out_spec],
        out_shape=[jax.ShapeDtypeStruct((n_seq, hw), F32)] * 2,
        compiler_params=_params(("parallel",)),
        name=f"attn_sample_d{dil}",
    )(z, z, z, kv)


def _merge_kernel(*refs, n_prompt_blocks):
    n_grp = len(ATTN_GROUPS)
    po, pl_, so, sl_ = (refs[0:n_grp], refs[n_grp:2 * n_grp],
                        refs[2 * n_grp:3 * n_grp], refs[3 * n_grp:4 * n_grp])
    out_ref = refs[4 * n_grp]
    i = pl.program_id(0)

    def merge(o_refs, l_refs):
        ls = [r[...] for r in l_refs]
        mx = functools.reduce(jnp.maximum, ls)
        es = [jnp.exp(l - mx) for l in ls]
        den = functools.reduce(lambda a, b: a + b, es)
        acc = (es[0] / den) * o_refs[0][...]
        for g in range(1, n_grp):
            acc = acc + (es[g] / den) * o_refs[g][...]
        out_ref[...] = acc.astype(out_ref.dtype)

    @pl.when(i < n_prompt_blocks)
    def _():
        merge(po, pl_)

    @pl.when(i >= n_prompt_blocks)
    def _():
        merge(so, sl_)


def _merge_groups(prompt_o, prompt_lse, sample_o, sample_lse):
    n_tok = prompt_o[0].shape[0]
    n_seq = sample_o[0].shape[0]
    hw = HEADS * HEAD_DIM
    npb = n_tok // TOK_BLOCK
    nsb = n_seq // TOK_BLOCK

    pspec = pl.BlockSpec((TOK_BLOCK, hw), lambda i: (jnp.minimum(i, npb - 1), 0))
    sspec = pl.BlockSpec((TOK_BLOCK, hw), lambda i: (jnp.maximum(i - npb, 0), 0))
    n_grp = len(ATTN_GROUPS)
    return pl.pallas_call(
        functools.partial(_merge_kernel, n_prompt_blocks=npb),
        grid=(npb + nsb,),
        in_specs=[pspec] * (2 * n_grp) + [sspec] * (2 * n_grp),
        out_specs=pl.BlockSpec((TOK_BLOCK, hw), lambda i: (i, 0)),
        out_shape=jax.ShapeDtypeStruct((n_tok + n_seq, hw), BF16),
        compiler_params=_params(("parallel",)),
        name="merge_groups",
    )(*prompt_o, *prompt_lse, *sample_o, *sample_lse)


def _pool_kernel(*refs, rows, n_prompt_blocks, n_seq):
    ng = len(POOL_WINDOWS)
    cur_refs, halo_refs = refs[0:ng], refs[ng:2 * ng]
    st_ref, w_ref, s_ref, o_ref = refs[2 * ng:2 * ng + 4]
    i = pl.program_id(0)
    pg = cur_refs[0].shape[1]

    @pl.when(i < n_prompt_blocks)
    def _():
        pos = i * rows + lax.broadcasted_iota(jnp.int32, (rows, 1), 0)
        for g, win in enumerate(POOL_WINDOWS):
            cur = cur_refs[g][...]
            halo = jnp.where(i > 0, halo_refs[g][...], 0.0)
            a = jnp.concatenate([halo, cur], axis=0)
            sh = 1
            while sh < win:
                a = a + pltpu.roll(a, sh, 0)
                sh *= 2
            cnt = jnp.minimum(pos + 1, win).astype(F32)
            d = a[POOL_HALO:, :] / cnt - cur
            sl = slice(g * pg, (g + 1) * pg)
            y = jnp.dot(d.astype(BF16), w_ref[g].astype(BF16), preferred_element_type=F32)
            o_ref[:, sl] = (y * s_ref[:, sl]).astype(o_ref.dtype)

    @pl.when(i == n_prompt_blocks)
    def _():
        ctx = st_ref.shape[1]
        for g, win in enumerate(POOL_WINDOWS):
            sl = slice(g * pg, (g + 1) * pg)
            cur = cur_refs[g][0:n_seq, :]
            acc = cur
            for jj in range(1, win):
                acc = acc + st_ref[:, ctx - jj, sl]
            cnt = float(min(PAST_LEN + 1, win))
            d = acc / cnt - cur
            y = jnp.dot(d.astype(BF16), w_ref[g].astype(BF16), preferred_element_type=F32)
            o_ref[0:n_seq, sl] = (y * s_ref[:, sl]).astype(o_ref.dtype)


def _pool_branch(z, state, w_pool, pool_scale, n_tok, n_seq, u_col0):
    m = z.shape[0]
    ng, pg, _ = w_pool.shape
    pw = ng * pg
    rows = 256
    assert n_seq <= rows and m == n_tok + n_seq
    c0 = u_col0 // pg
    scale = pool_scale.reshape(1, pw)
    hpb = rows // POOL_HALO
    npb = n_tok // rows

    cur_specs = [pl.BlockSpec((rows, pg), lambda i, g=g: (i, c0 + g)) for g in range(ng)]
    halo_specs = [pl.BlockSpec((POOL_HALO, pg),
                               lambda i, g=g: (jnp.maximum(i * hpb - 1, 0), c0 + g))
                  for g in range(ng)]
    return pl.pallas_call(
        functools.partial(_pool_kernel, rows=rows, n_prompt_blocks=npb, n_seq=n_seq),
        grid=(npb + 1,),
        in_specs=cur_specs + halo_specs + [
            pl.BlockSpec(state.shape, lambda i: (0, 0, 0)),
            pl.BlockSpec((ng, pg, pg), lambda i: (0, 0, 0)),
            pl.BlockSpec((1, pw), lambda i: (0, 0))],
        out_specs=pl.BlockSpec((rows, pw), lambda i: (i, 0)),
        out_shape=jax.ShapeDtypeStruct((m, pw), BF16),
        compiler_params=_params(("parallel",)),
        name="pool",
    )(*([z] * (2 * ng)), state, w_pool, scale)


def _mix_kernel(oa_ref, po_ref, wa_ref, wp_ref, ga_ref, gb_ref, o_ref, wab_ref, wpb_ref):
    @pl.when(pl.program_id(1) == 0)
    def _():
        wab_ref[...] = wa_ref[...].astype(BF16)
        wpb_ref[...] = wp_ref[...].astype(BF16)

    a = jnp.dot(oa_ref[...], wab_ref[...], preferred_element_type=F32)
    p = jnp.dot(po_ref[...], wpb_ref[...], preferred_element_type=F32)
    o_ref[...] = (ga_ref[...] * a + gb_ref[...] * p).astype(o_ref.dtype)


def _mix(o_attn, pool_out, w_a, w_p, z, gate_col0):
    m, ka = o_attn.shape
    kp = pool_out.shape[1]
    n = w_a.shape[1]
    ga0 = gate_col0 // COL_TILE
    gb0 = (gate_col0 + n) // COL_TILE
    return pl.pallas_call(
        _mix_kernel,
        grid=(n // COL_TILE, m // ROW_TILE),
        in_specs=[pl.BlockSpec((ROW_TILE, ka), lambda j, i: (i, 0)),
                  pl.BlockSpec((ROW_TILE, kp), lambda j, i: (i, 0)),
                  pl.BlockSpec((ka, COL_TILE), lambda j, i: (0, j)),
                  pl.BlockSpec((kp, COL_TILE), lambda j, i: (0, j)),
                  pl.BlockSpec((ROW_TILE, COL_TILE), lambda j, i: (i, ga0 + j)),
                  pl.BlockSpec((ROW_TILE, COL_TILE), lambda j, i: (i, gb0 + j))],
        out_specs=pl.BlockSpec((ROW_TILE, COL_TILE), lambda j, i: (i, j)),
        out_shape=jax.ShapeDtypeStruct((m, n), BF16),
        scratch_shapes=[pltpu.VMEM((ka, COL_TILE), BF16), pltpu.VMEM((kp, COL_TILE), BF16)],
        compiler_params=_params(("arbitrary", "arbitrary")),
        name="mix",
    )(o_attn, pool_out, w_a, w_p, z, z)


def _outproj_kernel(a_ref, w_ref, xp_ref, xs_ref, o_ref, wb_ref, *, split):
    i = pl.program_id(1)
    last = pl.num_programs(1) - 1

    @pl.when(i == 0)
    def _():
        wb_ref[...] = w_ref[...].astype(BF16)

    y = jnp.dot(a_ref[...], wb_ref[...], preferred_element_type=F32)

    @pl.when(i < last)
    def _():
        o_ref[...] = xp_ref[...] + y

    @pl.when(i == last)
    def _():
        o_ref[0:split, :] = xp_ref[0:split, :] + y[0:split, :]
        o_ref[split:, :] = xs_ref[...] + y[split:, :]


def _outproj(mixed, w_out, xp, xs):
    m, k = mixed.shape
    n = w_out.shape[1]
    n_tok, n_seq = xp.shape[0], xs.shape[0]
    nblk = m // ROW_TILE
    split = n_tok - (nblk - 1) * ROW_TILE
    assert 0 < split and split + n_seq == ROW_TILE and split % 8 == 0
    return pl.pallas_call(
        functools.partial(_outproj_kernel, split=split),
        grid=(n // COL_TILE, nblk),
        in_specs=[pl.BlockSpec((ROW_TILE, k), lambda j, i: (i, 0)),
                  pl.BlockSpec((k, COL_TILE), lambda j, i: (0, j)),
                  pl.BlockSpec((ROW_TILE, COL_TILE), lambda j, i: (i, j)),
                  pl.BlockSpec((n_seq, COL_TILE), lambda j, i: (0, j))],
        out_specs=pl.BlockSpec((ROW_TILE, COL_TILE), lambda j, i: (i, j)),
        out_shape=jax.ShapeDtypeStruct((m, n), F32),
        scratch_shapes=[pltpu.VMEM((k, COL_TILE), BF16)],
        compiler_params=_params(("arbitrary", "arbitrary")),
        name="outproj",
    )(mixed, w_out, xp, xs)


def _router_kernel(x_ref, g_ref, wr_ref, br_ref, h_ref, e_ref, r_ref, w_ref, c_ref, carry_ref):
    i = pl.program_id(0)

    @pl.when(i == 0)
    def _():
        carry_ref[...] = jnp.zeros_like(carry_ref)

    x = x_ref[...]
    ms = jnp.mean(x * x, axis=-1, keepdims=True)
    h = x * lax.rsqrt(ms + RMS_EPS) * g_ref[...]
    h_ref[...] = h
    logits = jnp.dot(h.astype(BF16), wr_ref[...].astype(BF16),
                     preferred_element_type=F32) + br_ref[...]
    tb, ne = logits.shape
    elane = lax.broadcasted_iota(jnp.int32, (tb, ne), 1).astype(F32)
    work = logits
    sel = jnp.zeros((tb, ne), F32)
    idxs, vals = [], []
    for _ in range(TOP_K):
        mx = jnp.max(work, axis=-1, keepdims=True)
        idx = jnp.min(jnp.where(work == mx, elane, float(ne)), axis=-1, keepdims=True)
        hit = elane == idx
        sel = jnp.where(hit, 1.0, sel)
        work = jnp.where(hit, -jnp.inf, work)
        idxs.append(idx)
        vals.append(mx)
    exps = [jnp.exp(v - vals[0]) for v in vals]
    den = functools.reduce(lambda a, b: a + b, exps)

    ri = lax.broadcasted_iota(jnp.int32, (tb, tb), 0)
    ci = lax.broadcasted_iota(jnp.int32, (tb, tb), 1)
    tril = jnp.where(ci < ri, 1.0, 0.0).astype(BF16)
    before = jnp.dot(tril, sel.astype(BF16), preferred_element_type=F32) + carry_ref[0:1, :]
    carry_ref[...] = carry_ref[...] + jnp.sum(sel, axis=0, keepdims=True)

    lane = lax.broadcasted_iota(jnp.int32, (tb, HEAD_DIM), 1)
    e_out = jnp.zeros((tb, HEAD_DIM), jnp.int32)
    r_out = jnp.zeros((tb, HEAD_DIM), jnp.int32)
    w_out = jnp.zeros((tb, HEAD_DIM), F32)
    for k in range(TOP_K):
        rank = jnp.sum(jnp.where(elane == idxs[k], before, 0.0), axis=-1, keepdims=True)
        e_out = jnp.where(lane == k, idxs[k].astype(jnp.int32), e_out)
        r_out = jnp.where(lane == k, rank.astype(jnp.int32), r_out)
        w_out = jnp.where(lane == k, exps[k] / den, w_out)
    e_ref[...] = e_out
    r_ref[...] = r_out
    w_ref[...] = w_out
    c_ref[...] = carry_ref[...].astype(jnp.int32)


def _router(x1, gain, w_router, b_router):
    m, d = x1.shape
    ne = w_router.shape[1]
    nblk = m // TOK_BLOCK
    lanes = HEAD_DIM
    return pl.pallas_call(
        _router_kernel,
        grid=(nblk,),
        in_specs=[pl.BlockSpec((TOK_BLOCK, d), lambda i: (i, 0)),
                  pl.BlockSpec((1, d), lambda i: (0, 0)),
                  pl.BlockSpec((d, ne), lambda i: (0, 0)),
                  pl.BlockSpec((1, ne), lambda i: (0, 0))],
        out_specs=[pl.BlockSpec((TOK_BLOCK, d), lambda i: (i, 0)),
                   pl.BlockSpec((TOK_BLOCK, lanes), lambda i: (i, 0)),
                   pl.BlockSpec((TOK_BLOCK, lanes), lambda i: (i, 0)),
                   pl.BlockSpec((TOK_BLOCK, lanes), lambda i: (i, 0)),
                   pl.BlockSpec((8, ne), lambda i: (0, 0))],
        out_shape=[jax.ShapeDtypeStruct((m, d), F32),
                   jax.ShapeDtypeStruct((m, lanes), jnp.int32),
                   jax.ShapeDtypeStruct((m, lanes), jnp.int32),
                   jax.ShapeDtypeStruct((m, lanes), F32),
                   jax.ShapeDtypeStruct((8, ne), jnp.int32)],
        scratch_shapes=[pltpu.VMEM((8, ne), F32)],
        compiler_params=_params(("arbitrary",)),
        name="router",
    )(x1, gain.reshape(1, d), w_router, b_router.reshape(1, ne))


def _dispatch_kernel(tok_ref, nxt_ref, h_hbm, o_ref, buf_ref, sem):
    u = pl.program_id(0)
    nu = pl.num_programs(0)
    rows = buf_ref.shape[1]
    slot = lax.rem(u, 2)

    def row_copy(idx_ref, r, s):
        return pltpu.make_async_copy(h_hbm.at[pl.ds(idx_ref[0, 0, r], 1), :],
                                     buf_ref.at[s, pl.ds(r, 1), :], sem.at[s])

    def request(idx_ref, s):
        def body(r2, c):
            row_copy(idx_ref, 2 * r2, s).start(priority=0)
            row_copy(idx_ref, 2 * r2 + 1, s).start(priority=1)
            return c
        lax.fori_loop(0, rows // 2, body, 0)

    @pl.when(u == 0)
    def _():
        request(tok_ref, slot)

    @pl.when(u + 1 < nu)
    def _():
        request(nxt_ref, 1 - slot)

    def wait(r, c):
        row_copy(tok_ref, r, slot).wait()
        return c

    lax.fori_loop(0, rows, wait, 0)
    o_ref[...] = buf_ref[slot].astype(o_ref.dtype)


def _dispatch(h, row_tok, n_rows):
    d = h.shape[1]
    nu = n_rows // EXPERT_TILE
    tok = row_tok.reshape(nu, 1, EXPERT_TILE)
    return pl.pallas_call(
        _dispatch_kernel,
        grid=(nu,),
        in_specs=[pl.BlockSpec((1, 1, EXPERT_TILE), lambda u: (u, 0, 0),
                               memory_space=pltpu.SMEM),
                  pl.BlockSpec((1, 1, EXPERT_TILE),
                               lambda u: (jnp.minimum(u + 1, nu - 1), 0, 0),
                               memory_space=pltpu.SMEM),
                  pl.BlockSpec(memory_space=pl.ANY)],
        out_specs=pl.BlockSpec((EXPERT_TILE, d), lambda u: (u, 0)),
        out_shape=jax.ShapeDtypeStruct((n_rows, d), BF16),
        scratch_shapes=[pltpu.VMEM((2, EXPERT_TILE, d), F32),
                        pltpu.SemaphoreType.DMA((2,))],
        compiler_params=_params(("arbitrary",)),
        name="dispatch",
    )(tok, tok, h)


def _stream_weight_tiles(w_hbm, col_offsets, land, wbf_refs, sem,
                         ue_ref, uf_ref, nxe_ref, nxw_ref):
    f = pl.program_id(0)
    u = pl.program_id(1)
    nf = pl.num_programs(0)
    parts = range(len(col_offsets))

    def tile_copy(e, fcol, part):
        col = pl.multiple_of(fcol * EXPERT_COLS + col_offsets[part], EXPERT_COLS)
        return pltpu.make_async_copy(w_hbm.at[0, e, :, pl.ds(col, EXPERT_COLS)],
                                     land.at[part], sem.at[part])

    @pl.when(jnp.logical_and(f == 0, u == 0))
    def _():
        for part in parts:
            tile_copy(ue_ref[0], 0, part).start()

    def cast_rows(part):
        def body(i, carry):
            r = pl.multiple_of(i * CAST_ROWS, CAST_ROWS)
            wbf_refs[part][pl.ds(r, CAST_ROWS), :] = (
                land[part, pl.ds(r, CAST_ROWS), :].astype(BF16))
            return carry
        lax.fori_loop(0, land.shape[1] // CAST_ROWS, body, 0)

    @pl.when(uf_ref[u] == 1)
    def _():
        for part in parts:
            tile_copy(ue_ref[u], f, part).wait()
            cast_rows(part)
        wraps = nxw_ref[u]

        @pl.when(jnp.logical_not(jnp.logical_and(wraps == 1, f == nf - 1)))
        def _():
            for part in parts:
                tile_copy(nxe_ref[u], f + wraps, part).start()


def _gate_up_kernel(ue_ref, ub_ref, uf_ref, nxe_ref, nxw_ref, na_ref, x_ref, w_hbm, bg_ref,
                    bu_ref, o_ref, land, wgb_ref, wub_ref, sem):
    del ub_ref
    u = pl.program_id(1)
    ff = pl.num_programs(0) * EXPERT_COLS
    _stream_weight_tiles(w_hbm, (0, ff), land, (wgb_ref, wub_ref), sem,
                         ue_ref, uf_ref, nxe_ref, nxw_ref)

    @pl.when(u < na_ref[0])
    def _():
        for c in range(EXPERT_COLS // COL_TILE):
            sl = slice(c * COL_TILE, (c + 1) * COL_TILE)
            gate = jnp.dot(x_ref[...], wgb_ref[:, sl],
                           preferred_element_type=F32) + bg_ref[:, sl]
            up = jnp.dot(x_ref[...], wub_ref[:, sl],
                         preferred_element_type=F32) + bu_ref[:, sl]
            gate = jnp.minimum(gate, SWIGLU_LIMIT)
            up = jnp.clip(up, -SWIGLU_LIMIT, SWIGLU_LIMIT)
            act = gate * jax.nn.sigmoid(SWIGLU_ALPHA * gate) * (up + 1.0)
            o_ref[:, sl] = act.astype(o_ref.dtype)

    @pl.when(u >= na_ref[0])
    def _():
        o_ref[...] = jnp.zeros_like(o_ref)


def _unit_imap(fn):
    return lambda c, u, ue, ub, uf, nxe, nxw, na: fn(c, u, ue, ub)


def _gate_up(xs, w_gate_up, b_gate_up, units):
    n_rows, d = xs.shape
    ne = w_gate_up.shape[1]
    ff = w_gate_up.shape[3] // 2
    nu = n_rows // EXPERT_TILE
    nf = ff // EXPERT_COLS
    bias = b_gate_up.reshape(ne, 1, 2 * ff)
    grid_spec = pltpu.PrefetchScalarGridSpec(
        num_scalar_prefetch=len(units),
        grid=(nf, nu),
        in_specs=[
            pl.BlockSpec((EXPERT_TILE, d), _unit_imap(lambda f, u, ue, ub: (ub[u], 0))),
            pl.BlockSpec(memory_space=pl.ANY),
            pl.BlockSpec((None, 1, EXPERT_COLS),
                         _unit_imap(lambda f, u, ue, ub: (ue[u], 0, f))),
            pl.BlockSpec((None, 1, EXPERT_COLS),
                         _unit_imap(lambda f, u, ue, ub: (ue[u], 0, nf + f))),
        ],
        out_specs=pl.BlockSpec((EXPERT_TILE, EXPERT_COLS),
                               _unit_imap(lambda f, u, ue, ub: (u, f))),
        scratch_shapes=[pltpu.VMEM((2, d, EXPERT_COLS), F32),
                        pltpu.VMEM((d, EXPERT_COLS), BF16),
                        pltpu.VMEM((d, EXPERT_COLS), BF16),
                        pltpu.SemaphoreType.DMA((2,))],
    )
    return pl.pallas_call(
        _gate_up_kernel,
        grid_spec=grid_spec,
        out_shape=jax.ShapeDtypeStruct((n_rows, ff), BF16),
        compiler_params=_params(("arbitrary", "arbitrary"), EXPERT_VMEM_LIMIT),
        name="expert_gate_up",
    )(*units, xs, w_gate_up, bias, bias)


def _down_kernel(ue_ref, ub_ref, uf_ref, nxe_ref, nxw_ref, na_ref, a_ref, w_hbm, b_ref,
                 o_ref, land, wb_ref, sem):
    del ub_ref
    u = pl.program_id(1)
    _stream_weight_tiles(w_hbm, (0,), land, (wb_ref,), sem,
                         ue_ref, uf_ref, nxe_ref, nxw_ref)

    @pl.when(u < na_ref[0])
    def _():
        o_ref[...] = jnp.dot(a_ref[...], wb_ref[...], preferred_element_type=F32) + b_ref[...]

    @pl.when(u >= na_ref[0])
    def _():
        o_ref[...] = jnp.zeros_like(o_ref)


def _down(act, w_down, b_down, units):
    n_rows, ff = act.shape
    ne = w_down.shape[1]
    d = w_down.shape[3]
    nu = n_rows // EXPERT_TILE
    bias = b_down.reshape(ne, 1, d)
    grid_spec = pltpu.PrefetchScalarGridSpec(
        num_scalar_prefetch=len(units),
        grid=(d // EXPERT_COLS, nu),
        in_specs=[
            pl.BlockSpec((EXPERT_TILE, ff), _unit_imap(lambda n, u, ue, ub: (ub[u], 0))),
            pl.BlockSpec(memory_space=pl.ANY),
            pl.BlockSpec((None, 1, EXPERT_COLS),
                         _unit_imap(lambda n, u, ue, ub: (ue[u], 0, n))),
        ],
        out_specs=pl.BlockSpec((EXPERT_TILE, EXPERT_COLS),
                               _unit_imap(lambda n, u, ue, ub: (u, n))),
        scratch_shapes=[pltpu.VMEM((1, ff, EXPERT_COLS), F32),
                        pltpu.VMEM((ff, EXPERT_COLS), BF16),
                        pltpu.SemaphoreType.DMA((1,))],
    )
    return pl.pallas_call(
        _down_kernel,
        grid_spec=grid_spec,
        out_shape=jax.ShapeDtypeStruct((n_rows, d), F32),
        compiler_params=_params(("arbitrary", "arbitrary")),
        name="expert_down",
    )(*units, act, w_down, bias)


def _combine_kernel(pos_ref, nxt_ref, x_ref, w_ref, y_hbm, op_ref, os_ref, buf_ref, sem, *,
                    n_prompt_blocks):
    tb = x_ref.shape[0]
    i = pl.program_id(0)
    nblk = pl.num_programs(0)
    slot = lax.rem(i, 2)

    def row_copy(idx_ref, t, k, s):
        return pltpu.make_async_copy(y_hbm.at[pl.ds(idx_ref[0, 0, t * TOP_K + k], 1), :],
                                     buf_ref.at[s, k, pl.ds(t, 1), :], sem.at[s])

    def request(idx_ref, s):
        def body(t, c):
            for k in range(TOP_K):
                row_copy(idx_ref, t, k, s).start(priority=k % 2)
            return c
        lax.fori_loop(0, tb, body, 0)

    @pl.when(i == 0)
    def _():
        request(pos_ref, slot)

    @pl.when(i + 1 < nblk)
    def _():
        request(nxt_ref, 1 - slot)

    def wait(t, c):
        for k in range(TOP_K):
            row_copy(pos_ref, t, k, slot).wait()
        return c

    lax.fori_loop(0, tb, wait, 0)
    w = w_ref[...]
    acc = w[:, 0:1] * buf_ref[slot, 0]
    for k in range(1, TOP_K):
        acc = acc + w[:, k:k + 1] * buf_ref[slot, k]
    y = x_ref[...] + acc

    @pl.when(i < n_prompt_blocks)
    def _():
        op_ref[...] = y

    @pl.when(i >= n_prompt_blocks)
    def _():
        os_ref[...] = y


def _combine(x1, wts, pos, y_sorted, n_tok, n_seq):
    m, d = x1.shape
    nblk = m // TOK_BLOCK
    npb = n_tok // TOK_BLOCK
    assert n_seq == TOK_BLOCK and npb + 1 == nblk
    posb = pos.reshape(nblk, 1, TOK_BLOCK * TOP_K)
    return pl.pallas_call(
        functools.partial(_combine_kernel, n_prompt_blocks=npb),
        grid=(nblk,),
        in_specs=[pl.BlockSpec((1, 1, TOK_BLOCK * TOP_K), lambda i: (i, 0, 0),
                               memory_space=pltpu.SMEM),
                  pl.BlockSpec((1, 1, TOK_BLOCK * TOP_K),
                               lambda i: (jnp.minimum(i + 1, nblk - 1), 0, 0),
                               memory_space=pltpu.SMEM),
                  pl.BlockSpec((TOK_BLOCK, d), lambda i: (i, 0)),
                  pl.BlockSpec((TOK_BLOCK, wts.shape[1]), lambda i: (i, 0)),
                  pl.BlockSpec(memory_space=pl.ANY)],
        out_specs=[pl.BlockSpec((TOK_BLOCK, d), lambda i: (jnp.minimum(i, npb - 1), 0)),
                   pl.BlockSpec((TOK_BLOCK, d), lambda i: (0, 0))],
        out_shape=[jax.ShapeDtypeStruct((n_tok, d), F32),
                   jax.ShapeDtypeStruct((n_seq, d), F32)],
        scratch_shapes=[pltpu.VMEM((2, TOP_K, TOK_BLOCK, d), F32),
                        pltpu.SemaphoreType.DMA((2,))],
        compiler_params=_params(("arbitrary",)),
        name="combine",
    )(posb, posb, x1, wts, y_sorted)


def _moe(x1, layer_w, n_tok):
    norm_ffn, w_router, b_router, w_gate_up, b_gate_up, w_down, b_down = layer_w
    m = x1.shape[0]
    ne = w_router.shape[1]
    h, eidx, rank, wts, counts = _router(x1, norm_ffn, w_router, b_router)

    tile = EXPERT_TILE
    counts = counts[0]
    padded = (counts + tile - 1) // tile * tile
    pend = jnp.cumsum(padded)
    pstart = pend - padded
    eidx = eidx[:, :TOP_K]
    pos = (pstart[eidx] + rank[:, :TOP_K]).astype(jnp.int32)
    n_rows = (m * TOP_K + ne * (tile - 1) + tile - 1) // tile * tile
    nu = n_rows // tile
    tok = jnp.broadcast_to(jnp.arange(m, dtype=jnp.int32)[:, None], pos.shape)
    row_tok = jnp.zeros((n_rows,), jnp.int32).at[pos.reshape(-1)].set(tok.reshape(-1))
    n_active = (pend[-1] // tile).astype(jnp.int32)
    u = jnp.arange(nu, dtype=jnp.int32)
    unit_blk = jnp.minimum(u, n_active - 1)
    unit_e = jnp.sum((pend[None, :] <= (unit_blk * tile)[:, None]).astype(jnp.int32), axis=1)
    unit_e = jnp.minimum(unit_e, ne - 1)
    prev_e = jnp.concatenate([jnp.full((1,), -1, jnp.int32), unit_e[:-1]])
    first = (u < n_active) & (unit_e != prev_e)
    unit_first = first.astype(jnp.int32)
    later = first[None, :] & (u[None, :] > u[:, None])
    nxt = jnp.min(jnp.where(later, u[None, :], nu), axis=1)
    next_wraps = (nxt >= nu).astype(jnp.int32)
    next_e = jnp.where(nxt >= nu, unit_e[0], unit_e[jnp.minimum(nxt, nu - 1)])
    units = (unit_e, unit_blk, unit_first, next_e.astype(jnp.int32), next_wraps,
             n_active.reshape(1))

    xs = _dispatch(h, row_tok, n_rows)
    act = _gate_up(xs, w_gate_up, b_gate_up, units)
    ys = _down(act, w_down, b_down, units)
    return _combine(x1, wts, pos, ys, n_tok, m - n_tok)


def kernel(x_prompt, x_sample, cache_kv_w128, cache_kv_w512, cache_kv_w2048, state_pool,
           norm_mix, w_in, q_norm, k_norm, w_pool_group, pool_scale, w_branch_attn,
           w_branch_pool, w_out, norm_ffn, w_router, b_router, w_gate_up, b_gate_up,
           w_down, b_down):
    depth = norm_mix.shape[0]
    assert depth == 1 and x_prompt.shape[0] == 1 and x_sample.shape[1] == 1
    n_tok, d_model = x_prompt.shape[1], x_prompt.shape[2]
    n_seq = x_sample.shape[0]
    caches = (cache_kv_w128, cache_kv_w512, cache_kv_w2048)
    n_grp = len(ATTN_GROUPS)
    hw = HEADS * HEAD_DIM
    qkv_w = n_grp * hw
    pool_w = w_pool_group.shape[1] * w_pool_group.shape[2]
    width = w_in.shape[2]
    layer = 0

    xp, xs = x_prompt[0], x_sample[:, 0]

    ones = jnp.ones((width - 2 * qkv_w,), F32)
    colgain = jnp.concatenate([
        jnp.broadcast_to(q_norm[layer][:, None, :], (n_grp, HEADS, HEAD_DIM)).reshape(-1),
        jnp.broadcast_to(k_norm[layer][:, None, :], (n_grp, HEADS, HEAD_DIM)).reshape(-1),
        ones]).reshape(1, width)

    h = _rmsnorm(xp, xs, norm_mix[layer])
    z = _inproj(h, w_in[layer], colgain, 2 * qkv_w, qkv_w + pool_w)

    po, plse, so, slse = [], [], [], []
    for g, (window, dil) in enumerate(ATTN_GROUPS):
        assert window == BAND * dil
        o, lse = _attn_prompt(z, g, dil, n_tok)
        po.append(o)
        plse.append(lse)
        o, lse = _attn_sample(z, caches[g][layer], g, dil, n_tok, n_seq)
        so.append(o)
        slse.append(lse)
    o_attn = _merge_groups(po, plse, so, slse)

    pool_out = _pool_branch(z, state_pool[layer], w_pool_group[layer], pool_scale[layer],
                            n_tok, n_seq, 3 * qkv_w)
    mixed = _mix(o_attn, pool_out, w_branch_attn[layer], w_branch_pool[layer], z,
                 3 * qkv_w + pool_w)
    x1 = _outproj(mixed, w_out[layer], xp, xs)
    y_p, y_s = _moe(x1, (norm_ffn[layer], w_router[layer], b_router[layer],
                         w_gate_up, b_gate_up, w_down, b_down), n_tok)
    y_prompt = y_p[None]
    y_sample = y_s[:, None]

    outs = [y_prompt, y_sample]
    k_all = z[:, qkv_w:2 * qkv_w]
    v_all = z[:, 2 * qkv_w:3 * qkv_w]
    for g, (window, dil) in enumerate(ATTN_GROUPS):
        keep = min(window, n_tok)
        sl = slice(g * hw, (g + 1) * hw)
        kp = k_all[n_tok - keep:n_tok, sl].reshape(keep, HEADS, HEAD_DIM)
        vp = v_all[n_tok - keep:n_tok, sl].reshape(keep, HEADS, HEAD_DIM)
        outs.append(jnp.stack([kp, vp], axis=0)[None, None])
        ks = k_all[n_tok:, sl].reshape(n_seq, 1, HEADS, HEAD_DIM)
        vs = v_all[n_tok:, sl].reshape(n_seq, 1, HEADS, HEAD_DIM)
        outs.append(jnp.stack([ks, vs], axis=1)[None])
    u_all = z[:, 3 * qkv_w:3 * qkv_w + pool_w]
    keep = min(POOL_HALO - 1, n_tok)
    outs.append(u_all[n_tok - keep:n_tok][None, None])
    outs.append(u_all[n_tok:][:, None][None])
    return tuple(outs)
```

```python
import functools

import jax
import jax.numpy as jnp
from jax import lax
from jax.experimental import pallas as pl
from jax.experimental.pallas import tpu as pltpu

F32 = jnp.float32
BF16 = jnp.bfloat16

HEAD_DIM = 128
SUBLANES = 8
HEADS = 8
ATTN_GROUPS = ((128, 1), (512, 4), (2048, 16))
BAND = 128
ATTN_ROWS = 2048
POOL_WINDOWS = (2, 4, 8, 16)
POOL_HALO = 16
TOP_K = 4
SWIGLU_LIMIT = 7.0
SWIGLU_ALPHA = 1.702
RMS_EPS = 1e-5
NEG_INF = -1e30
PAST_LEN = 2048

ROW_TILE = 640
COL_TILE = 512
TOK_BLOCK = 128
EXPERT_TILE = 256
EXPERT_COLS = 1024
CAST_ROWS = 256
VMEM_LIMIT = 56 * 1024 * 1024
EXPERT_VMEM_LIMIT = 62 * 1024 * 1024


def _params(semantics, vmem=VMEM_LIMIT):
    return pltpu.CompilerParams(dimension_semantics=semantics, vmem_limit_bytes=vmem)


def _rmsnorm_kernel(xp_ref, xs_ref, g_ref, o_ref, *, n_prompt_blocks):
    i = pl.program_id(0)

    def norm(x_ref):
        x = x_ref[...]
        ms = jnp.mean(x * x, axis=-1, keepdims=True)
        o_ref[...] = (x * lax.rsqrt(ms + RMS_EPS) * g_ref[...]).astype(o_ref.dtype)

    @pl.when(i < n_prompt_blocks)
    def _():
        norm(xp_ref)

    @pl.when(i >= n_prompt_blocks)
    def _():
        norm(xs_ref)


def _rmsnorm(xp, xs, gain):
    n_tok, d = xp.shape
    n_seq = xs.shape[0]
    npb, nsb = n_tok // TOK_BLOCK, n_seq // TOK_BLOCK
    return pl.pallas_call(
        functools.partial(_rmsnorm_kernel, n_prompt_blocks=npb),
        grid=(npb + nsb,),
        in_specs=[pl.BlockSpec((TOK_BLOCK, d), lambda i: (jnp.minimum(i, npb - 1), 0)),
                  pl.BlockSpec((TOK_BLOCK, d), lambda i: (jnp.maximum(i - npb, 0), 0)),
                  pl.BlockSpec((1, d), lambda i: (0, 0))],
        out_specs=pl.BlockSpec((TOK_BLOCK, d), lambda i: (i, 0)),
        out_shape=jax.ShapeDtypeStruct((n_tok + n_seq, d), BF16),
        compiler_params=_params(("parallel",)),
        name="rmsnorm",
    )(xp, xs, gain.reshape(1, d))


def _inproj_kernel(h_ref, w_ref, cg_ref, o_ref, wb_ref, *, qk_tiles, plain_end):
    j = pl.program_id(0)
    i = pl.program_id(1)

    @pl.when(i == 0)
    def _():
        wb_ref[...] = w_ref[...].astype(BF16)

    acc = jnp.dot(h_ref[...], wb_ref[...], preferred_element_type=F32)

    @pl.when(j < qk_tiles)
    def _():
        for hh in range(acc.shape[1] // HEAD_DIM):
            sl = slice(hh * HEAD_DIM, (hh + 1) * HEAD_DIM)
            blk = acc[:, sl]
            ms = jnp.mean(blk * blk, axis=-1, keepdims=True)
            o_ref[:, sl] = blk * lax.rsqrt(ms + RMS_EPS) * cg_ref[:, sl]

    @pl.when(jnp.logical_and(j >= qk_tiles, j < plain_end))
    def _():
        o_ref[...] = acc

    @pl.when(j >= plain_end)
    def _():
        o_ref[...] = jax.nn.sigmoid(acc)


def _inproj(h, w_in, colgain, qk_width, plain_width):
    m, d = h.shape
    n = w_in.shape[1]
    kern = functools.partial(_inproj_kernel, qk_tiles=qk_width // COL_TILE,
                             plain_end=(qk_width + plain_width) // COL_TILE)
    return pl.pallas_call(
        kern,
        grid=(n // COL_TILE, m // ROW_TILE),
        in_specs=[pl.BlockSpec((ROW_TILE, d), lambda j, i: (i, 0)),
                  pl.BlockSpec((d, COL_TILE), lambda j, i: (0, j)),
                  pl.BlockSpec((1, COL_TILE), lambda j, i: (0, j))],
        out_specs=pl.BlockSpec((ROW_TILE, COL_TILE), lambda j, i: (i, j)),
        out_shape=jax.ShapeDtypeStruct((m, n), F32),
        scratch_shapes=[pltpu.VMEM((d, COL_TILE), BF16)],
        compiler_params=_params(("arbitrary", "arbitrary")),
        name="inproj",
    )(h, w_in, colgain)


def _attn_prompt_kernel(q_ref, kh_ref, k_ref, vh_ref, v_ref, o_ref, lse_ref, *, dil):
    n = pl.program_id(1)
    span = BAND * dil
    qi = lax.broadcasted_iota(jnp.int32, (BAND, 2 * BAND), 0) + BAND
    ki = lax.broadcasted_iota(jnp.int32, (BAND, 2 * BAND), 1)
    dist = qi - ki
    band = (dist >= 0) & (dist <= BAND)
    first_key = jnp.where(n > 0, 0, BAND)
    band_first = band & (ki >= first_key)
    scale = HEAD_DIM ** -0.5

    def rows(ref, start, size):
        if dil == 1:
            return ref[pl.ds(start, size), :]
        return ref[pl.ds(start, size, stride=dil), :]

    for c in range(ATTN_ROWS // span):
        for r in range(dil):
            base = c * span + r
            q = rows(q_ref, base, BAND).astype(BF16)
            if c == 0:
                kc = jnp.concatenate([rows(kh_ref, r, BAND), rows(k_ref, r, BAND)], axis=0)
                vc = jnp.concatenate([rows(vh_ref, r, BAND), rows(v_ref, r, BAND)], axis=0)
                valid = band_first
            else:
                kc = rows(k_ref, base - span, 2 * BAND)
                vc = rows(v_ref, base - span, 2 * BAND)
                valid = band
            s = lax.dot_general(q, kc.astype(BF16), (((1,), (1,)), ((), ())),
                                preferred_element_type=F32) * scale
            s = jnp.where(valid, s, NEG_INF)
            m = jnp.max(s, axis=-1, keepdims=True)
            p = jnp.exp(s - m)
            l = jnp.sum(p, axis=-1, keepdims=True)
            o = jnp.dot(p.astype(BF16), vc.astype(BF16), preferred_element_type=F32) / l
            lse = jnp.broadcast_to(m + jnp.log(l), (BAND, HEAD_DIM))
            if dil == 1:
                o_ref[pl.ds(base, BAND), :] = o
                lse_ref[pl.ds(base, BAND), :] = lse
            else:
                o_ref[pl.ds(base, BAND, stride=dil), :] = o
                lse_ref[pl.ds(base, BAND, stride=dil), :] = lse


def _attn_prompt(z, g, dil, n_tok):
    hw = HEADS * HEAD_DIM
    n_grp = len(ATTN_GROUPS)
    span = BAND * dil
    spb = ATTN_ROWS // span
    qcol, kcol, vcol = (g * HEADS, (n_grp + g) * HEADS, (2 * n_grp + g) * HEADS)

    def cur(col):
        return pl.BlockSpec((ATTN_ROWS, HEAD_DIM), lambda h, n: (n, col + h))

    def halo(col):
        return pl.BlockSpec((span, HEAD_DIM),
                            lambda h, n: (jnp.maximum(n * spb - 1, 0), col + h))

    out_spec = pl.BlockSpec((ATTN_ROWS, HEAD_DIM), lambda h, n: (n, h))
    return pl.pallas_call(
        functools.partial(_attn_prompt_kernel, dil=dil),
        grid=(HEADS, n_tok // ATTN_ROWS),
        in_specs=[cur(qcol), halo(kcol), cur(kcol), halo(vcol), cur(vcol)],
        out_specs=[out_spec, out_spec],
        out_shape=[jax.ShapeDtypeStruct((n_tok, hw), F32)] * 2,
        compiler_params=_params(("parallel", "parallel")),
        name=f"attn_prompt_d{dil}",
    )(z, z, z, z, z)


def _attn_sample_kernel(q_ref, kn_ref, vn_ref, kv_ref, o_ref, lse_ref, *, rows):
    scale = HEAD_DIM ** -0.5

    def to_heads(row):
        return jnp.concatenate(
            [row[:, h * HEAD_DIM:(h + 1) * HEAD_DIM] for h in range(HEADS)], axis=0)

    def to_row(x):
        return jnp.concatenate([x[h:h + 1, :] for h in range(HEADS)], axis=1)

    o_rows, lse_rows = [], []
    for b in range(rows):
        q = to_heads(q_ref[b:b + 1, :])
        kn = to_heads(kn_ref[b:b + 1, :])
        vn = to_heads(vn_ref[b:b + 1, :])
        k = kv_ref[b, 0]
        v = kv_ref[b, 1]
        s = jnp.sum(k * q[None], axis=-1, keepdims=True) * scale
        sn = jnp.sum(q * kn, axis=-1, keepdims=True) * scale
        m = jnp.maximum(jnp.max(s, axis=0), sn)
        p = jnp.exp(s - m[None])
        pn = jnp.exp(sn - m)
        l = jnp.sum(p, axis=0) + pn
        o = (jnp.sum(p * v, axis=0) + pn * vn) / l
        lse = jnp.broadcast_to(m + jnp.log(l), (HEADS, HEAD_DIM))
        o_rows.append(to_row(o))
        lse_rows.append(to_row(lse))
    o_ref[...] = jnp.concatenate(o_rows, axis=0)
    lse_ref[...] = jnp.concatenate(lse_rows, axis=0)


def _attn_sample(z, cache, g, dil, row0, n_seq):
    hw = HEADS * HEAD_DIM
    n_grp = len(ATTN_GROUPS)
    length = cache.shape[2]
    assert length == BAND * dil, "cache must hold exactly one window"
    kv = cache.reshape(n_seq, 2, BAND, dil, HEADS, HEAD_DIM)
    rows = 8
    blk0 = row0 // rows

    def zspec(col):
        return pl.BlockSpec((rows, hw), lambda i: (blk0 + i, col))

    out_spec = pl.BlockSpec((rows, hw), lambda i: (i, 0))
    return pl.pallas_call(
        functools.partial(_attn_sample_kernel, rows=rows),
        grid=(n_seq // rows,),
        in_specs=[zspec(g), zspec(n_grp + g), zspec(2 * n_grp + g),
                  pl.BlockSpec((rows, 2, BAND, None, HEADS, HEAD_DIM),
                               lambda i: (i, 0, 0, 0, 0, 0))],
        out_specs=[out_spec, out_spec],
        out_shape=[jax.ShapeDtypeStruct((n_seq, hw), F32)] * 2,
        compiler_params=_params(("parallel",)),
        name=f"attn_sample_d{dil}",
    )(z, z, z, kv)


def _merge_kernel(*refs, n_prompt_blocks):
    n_grp = len(ATTN_GROUPS)
    po, pl_, so, sl_ = (refs[0:n_grp], refs[n_grp:2 * n_grp],
                        refs[2 * n_grp:3 * n_grp], refs[3 * n_grp:4 * n_grp])
    out_ref = refs[4 * n_grp]
    i = pl.program_id(0)

    def merge(o_refs, l_refs):
        ls = [r[...] for r in l_refs]
        mx = functools.reduce(jnp.maximum, ls)
        es = [jnp.exp(l - mx) for l in ls]
        den = functools.reduce(lambda a, b: a + b, es)
        acc = (es[0] / den) * o_refs[0][...]
        for g in range(1, n_grp):
            acc = acc + (es[g] / den) * o_refs[g][...]
        out_ref[...] = acc.astype(out_ref.dtype)

    @pl.when(i < n_prompt_blocks)
    def _():
        merge(po, pl_)

    @pl.when(i >= n_prompt_blocks)
    def _():
        merge(so, sl_)


def _merge_groups(prompt_o, prompt_lse, sample_o, sample_lse):
    n_tok = prompt_o[0].shape[0]
    n_seq = sample_o[0].shape[0]
    hw = HEADS * HEAD_DIM
    npb = n_tok // TOK_BLOCK
    nsb = n_seq // TOK_BLOCK

    pspec = pl.BlockSpec((TOK_BLOCK, hw), lambda i: (jnp.minimum(i, npb - 1), 0))
    sspec = pl.BlockSpec((TOK_BLOCK, hw), lambda i: (jnp.maximum(i - npb, 0), 0))
    n_grp = len(ATTN_GROUPS)
    return pl.pallas_call(
        functools.partial(_merge_kernel, n_prompt_blocks=npb),
        grid=(npb + nsb,),
        in_specs=[pspec] * (2 * n_grp) + [sspec] * (2 * n_grp),
        out_specs=pl.BlockSpec((TOK_BLOCK, hw), lambda i: (i, 0)),
        out_shape=jax.ShapeDtypeStruct((n_tok + n_seq, hw), BF16),
        compiler_params=_params(("parallel",)),
        name="merge_groups",
    )(*prompt_o, *prompt_lse, *sample_o, *sample_lse)


def _pool_kernel(*refs, rows, n_prompt_blocks, n_seq):
    ng = len(POOL_WINDOWS)
    cur_refs, halo_refs = refs[0:ng], refs[ng:2 * ng]
    st_ref, w_ref, s_ref, o_ref = refs[2 * ng:2 * ng + 4]
    i = pl.program_id(0)
    pg = cur_refs[0].shape[1]

    @pl.when(i < n_prompt_blocks)
    def _():
        pos = i * rows + lax.broadcasted_iota(jnp.int32, (rows, 1), 0)
        for g, win in enumerate(POOL_WINDOWS):
            cur = cur_refs[g][...]
            halo = jnp.where(i > 0, halo_refs[g][...], 0.0)
            a = jnp.concatenate([halo, cur], axis=0)
            sh = 1
            while sh < win:
                a = a + pltpu.roll(a, sh, 0)
                sh *= 2
            cnt = jnp.minimum(pos + 1, win).astype(F32)
            d = a[POOL_HALO:, :] / cnt - cur
            sl = slice(g * pg, (g + 1) * pg)
            y = jnp.dot(d.astype(BF16), w_ref[g].astype(BF16), preferred_element_type=F32)
            o_ref[:, sl] = (y * s_ref[:, sl]).astype(o_ref.dtype)

    @pl.when(i == n_prompt_blocks)
    def _():
        ctx = st_ref.shape[1]
        for g, win in enumerate(POOL_WINDOWS):
            sl = slice(g * pg, (g + 1) * pg)
            cur = cur_refs[g][0:n_seq, :]
            acc = cur
            for jj in range(1, win):
                acc = acc + st_ref[:, ctx - jj, sl]
            cnt = float(min(PAST_LEN + 1, win))
            d = acc / cnt - cur
            y = jnp.dot(d.astype(BF16), w_ref[g].astype(BF16), preferred_element_type=F32)
            o_ref[0:n_seq, sl] = (y * s_ref[:, sl]).astype(o_ref.dtype)


def _pool_branch(z, state, w_pool, pool_scale, n_tok, n_seq, u_col0):
    m = z.shape[0]
    ng, pg, _ = w_pool.shape
    pw = ng * pg
    rows = 256
    assert n_seq <= rows and m == n_tok + n_seq
    c0 = u_col0 // pg
    scale = pool_scale.reshape(1, pw)
    hpb = rows // POOL_HALO
    npb = n_tok // rows

    cur_specs = [pl.BlockSpec((rows, pg), lambda i, g=g: (i, c0 + g)) for g in range(ng)]
    halo_specs = [pl.BlockSpec((POOL_HALO, pg),
                               lambda i, g=g: (jnp.maximum(i * hpb - 1, 0), c0 + g))
                  for g in range(ng)]
    return pl.pallas_call(
        functools.partial(_pool_kernel, rows=rows, n_prompt_blocks=npb, n_seq=n_seq),
        grid=(npb + 1,),
        in_specs=cur_specs + halo_specs + [
            pl.BlockSpec(state.shape, lambda i: (0, 0, 0)),
            pl.BlockSpec((ng, pg, pg), lambda i: (0, 0, 0)),
            pl.BlockSpec((1, pw), lambda i: (0, 0))],
        out_specs=pl.BlockSpec((rows, pw), lambda i: (i, 0)),
        out_shape=jax.ShapeDtypeStruct((m, pw), BF16),
        compiler_params=_params(("parallel",)),
        name="pool",
    )(*([z] * (2 * ng)), state, w_pool, scale)


def _mix_kernel(oa_ref, po_ref, wa_ref, wp_ref, ga_ref, gb_ref, o_ref, wab_ref, wpb_ref):
    @pl.when(pl.program_id(1) == 0)
    def _():
        wab_ref[...] = wa_ref[...].astype(BF16)
        wpb_ref[...] = wp_ref[...].astype(BF16)

    a = jnp.dot(oa_ref[...], wab_ref[...], preferred_element_type=F32)
    p = jnp.dot(po_ref[...], wpb_ref[...], preferred_element_type=F32)
    o_ref[...] = (ga_ref[...] * a + gb_ref[...] * p).astype(o_ref.dtype)


def _mix(o_attn, pool_out, w_a, w_p, z, gate_col0):
    m, ka = o_attn.shape
    kp = pool_out.shape[1]
    n = w_a.shape[1]
    ga0 = gate_col0 // COL_TILE
    gb0 = (gate_col0 + n) // COL_TILE
    return pl.pallas_call(
        _mix_kernel,
        grid=(n // COL_TILE, m // ROW_TILE),
        in_specs=[pl.BlockSpec((ROW_TILE, ka), lambda j, i: (i, 0)),
                  pl.BlockSpec((ROW_TILE, kp), lambda j, i: (i, 0)),
                  pl.BlockSpec((ka, COL_TILE), lambda j, i: (0, j)),
                  pl.BlockSpec((kp, COL_TILE), lambda j, i: (0, j)),
                  pl.BlockSpec((ROW_TILE, COL_TILE), lambda j, i: (i, ga0 + j)),
                  pl.BlockSpec((ROW_TILE, COL_TILE), lambda j, i: (i, gb0 + j))],
        out_specs=pl.BlockSpec((ROW_TILE, COL_TILE), lambda j, i: (i, j)),
        out_shape=jax.ShapeDtypeStruct((m, n), BF16),
        scratch_shapes=[pltpu.VMEM((ka, COL_TILE), BF16), pltpu.VMEM((kp, COL_TILE), BF16)],
        compiler_params=_params(("arbitrary", "arbitrary")),
        name="mix",
    )(o_attn, pool_out, w_a, w_p, z, z)


def _outproj_kernel(a_ref, w_ref, xp_ref, xs_ref, o_ref, wb_ref, *, split):
    i = pl.program_id(1)
    last = pl.num_programs(1) - 1

    @pl.when(i == 0)
    def _():
        wb_ref[...] = w_ref[...].astype(BF16)

    y = jnp.dot(a_ref[...], wb_ref[...], preferred_element_type=F32)

    @pl.when(i < last)
    def _():
        o_ref[...] = xp_ref[...] + y

    @pl.when(i == last)
    def _():
        o_ref[0:split, :] = xp_ref[0:split, :] + y[0:split, :]
        o_ref[split:, :] = xs_ref[...] + y[split:, :]


def _outproj(mixed, w_out, xp, xs):
    m, k = mixed.shape
    n = w_out.shape[1]
    n_tok, n_seq = xp.shape[0], xs.shape[0]
    nblk = m // ROW_TILE
    split = n_tok - (nblk - 1) * ROW_TILE
    assert 0 < split and split + n_seq == ROW_TILE and split % 8 == 0
    return pl.pallas_call(
        functools.partial(_outproj_kernel, split=split),
        grid=(n // COL_TILE, nblk),
        in_specs=[pl.BlockSpec((ROW_TILE, k), lambda j, i: (i, 0)),
                  pl.BlockSpec((k, COL_TILE), lambda j, i: (0, j)),
                  pl.BlockSpec((ROW_TILE, COL_TILE), lambda j, i: (i, j)),
                  pl.BlockSpec((n_seq, COL_TILE), lambda j, i: (0, j))],
        out_specs=pl.BlockSpec((ROW_TILE, COL_TILE), lambda j, i: (i, j)),
        out_shape=jax.ShapeDtypeStruct((m, n), F32),
        scratch_shapes=[pltpu.VMEM((k, COL_TILE), BF16)],
        compiler_params=_params(("arbitrary", "arbitrary")),
        name="outproj",
    )(mixed, w_out, xp, xs)


def _router_kernel(x_ref, g_ref, wr_ref, br_ref, h_ref, e_ref, r_ref, w_ref, c_ref, carry_ref):
    i = pl.program_id(0)

    @pl.when(i == 0)
    def _():
        carry_ref[...] = jnp.zeros_like(carry_ref)

    x = x_ref[...]
    ms = jnp.mean(x * x, axis=-1, keepdims=True)
    h = x * lax.rsqrt(ms + RMS_EPS) * g_ref[...]
    h_ref[...] = h
    logits = jnp.dot(h.astype(BF16), wr_ref[...].astype(BF16),
                     preferred_element_type=F32) + br_ref[...]
    tb, ne = logits.shape
    elane = lax.broadcasted_iota(jnp.int32, (tb, ne), 1).astype(F32)
    work = logits
    sel = jnp.zeros((tb, ne), F32)
    idxs, vals = [], []
    for _ in range(TOP_K):
        mx = jnp.max(work, axis=-1, keepdims=True)
        idx = jnp.min(jnp.where(work == mx, elane, float(ne)), axis=-1, keepdims=True)
        hit = elane == idx
        sel = jnp.where(hit, 1.0, sel)
        work = jnp.where(hit, -jnp.inf, work)
        idxs.append(idx)
        vals.append(mx)
    exps = [jnp.exp(v - vals[0]) for v in vals]
    den = functools.reduce(lambda a, b: a + b, exps)

    ri = lax.broadcasted_iota(jnp.int32, (tb, tb), 0)
    ci = lax.broadcasted_iota(jnp.int32, (tb, tb), 1)
    tril = jnp.where(ci < ri, 1.0, 0.0).astype(BF16)
    before = jnp.dot(tril, sel.astype(BF16), preferred_element_type=F32) + carry_ref[0:1, :]
    carry_ref[...] = carry_ref[...] + jnp.sum(sel, axis=0, keepdims=True)

    lane = lax.broadcasted_iota(jnp.int32, (tb, HEAD_DIM), 1)
    e_out = jnp.zeros((tb, HEAD_DIM), jnp.int32)
    r_out = jnp.zeros((tb, HEAD_DIM), jnp.int32)
    w_out = jnp.zeros((tb, HEAD_DIM), F32)
    for k in range(TOP_K):
        rank = jnp.sum(jnp.where(elane == idxs[k], before, 0.0), axis=-1, keepdims=True)
        e_out = jnp.where(lane == k, idxs[k].astype(jnp.int32), e_out)
        r_out = jnp.where(lane == k, rank.astype(jnp.int32), r_out)
        w_out = jnp.where(lane == k, exps[k] / den, w_out)
    e_ref[...] = e_out
    r_ref[...] = r_out
    w_ref[...] = w_out
    c_ref[...] = carry_ref[...].astype(jnp.int32)


def _router(x1, gain, w_router, b_router):
    m, d = x1.shape
    ne = w_router.shape[1]
    nblk = m // TOK_BLOCK
    lanes = HEAD_DIM
    return pl.pallas_call(
        _router_kernel,
        grid=(nblk,),
        in_specs=[pl.BlockSpec((TOK_BLOCK, d), lambda i: (i, 0)),
                  pl.BlockSpec((1, d), lambda i: (0, 0)),
                  pl.BlockSpec((d, ne), lambda i: (0, 0)),
                  pl.BlockSpec((1, ne), lambda i: (0, 0))],
        out_specs=[pl.BlockSpec((TOK_BLOCK, d), lambda i: (i, 0)),
                   pl.BlockSpec((TOK_BLOCK, lanes), lambda i: (i, 0)),
                   pl.BlockSpec((TOK_BLOCK, lanes), lambda i: (i, 0)),
                   pl.BlockSpec((TOK_BLOCK, lanes), lambda i: (i, 0)),
                   pl.BlockSpec((8, ne), lambda i: (0, 0))],
        out_shape=[jax.ShapeDtypeStruct((m, d), F32),
                   jax.ShapeDtypeStruct((m, lanes), jnp.int32),
                   jax.ShapeDtypeStruct((m, lanes), jnp.int32),
                   jax.ShapeDtypeStruct((m, lanes), F32),
                   jax.ShapeDtypeStruct((8, ne), jnp.int32)],
        scratch_shapes=[pltpu.VMEM((8, ne), F32)],
        compiler_params=_params(("arbitrary",)),
        name="router",
    )(x1, gain.reshape(1, d), w_router, b_router.reshape(1, ne))


def _dispatch_kernel(tok_ref, nxt_ref, h_hbm, o_ref, buf_ref, sem):
    u = pl.program_id(0)
    nu = pl.num_programs(0)
    rows = buf_ref.shape[1]
    slot = lax.rem(u, 2)

    def row_copy(idx_ref, r, s):
        return pltpu.make_async_copy(h_hbm.at[pl.ds(idx_ref[0, 0, r], 1), :],
                                     buf_ref.at[s, pl.ds(r, 1), :], sem.at[s])

    groups = rows // SUBLANES

    def spread(i):
        return lax.rem(i, groups) * SUBLANES + i // groups

    def request(idx_ref, s):
        def body(r2, c):
            row_copy(idx_ref, spread(2 * r2), s).start(priority=0)
            row_copy(idx_ref, spread(2 * r2 + 1), s).start(priority=1)
            return c
        lax.fori_loop(0, rows // 2, body, 0)

    @pl.when(u == 0)
    def _():
        request(tok_ref, slot)

    @pl.when(u + 1 < nu)
    def _():
        request(nxt_ref, 1 - slot)

    def wait(r, c):
        row_copy(tok_ref, r, slot).wait()
        return c

    lax.fori_loop(0, rows, wait, 0)
    o_ref[...] = buf_ref[slot].astype(o_ref.dtype)


def _dispatch(h, row_tok, n_rows):
    d = h.shape[1]
    nu = n_rows // EXPERT_TILE
    tok = row_tok.reshape(nu, 1, EXPERT_TILE)
    return pl.pallas_call(
        _dispatch_kernel,
        grid=(nu,),
        in_specs=[pl.BlockSpec((1, 1, EXPERT_TILE), lambda u: (u, 0, 0),
                               memory_space=pltpu.SMEM),
                  pl.BlockSpec((1, 1, EXPERT_TILE),
                               lambda u: (jnp.minimum(u + 1, nu - 1), 0, 0),
                               memory_space=pltpu.SMEM),
                  pl.BlockSpec(memory_space=pl.ANY)],
        out_specs=pl.BlockSpec((EXPERT_TILE, d), lambda u: (u, 0)),
        out_shape=jax.ShapeDtypeStruct((n_rows, d), BF16),
        scratch_shapes=[pltpu.VMEM((2, EXPERT_TILE, d), F32),
                        pltpu.SemaphoreType.DMA((2,))],
        compiler_params=_params(("arbitrary",)),
        name="dispatch",
    )(tok, tok, h)


def _stream_weight_tiles(w_hbm, col_offsets, land, wbf_refs, sem,
                         ue_ref, uf_ref, nxe_ref, nxw_ref):
    f = pl.program_id(0)
    u = pl.program_id(1)
    nf = pl.num_programs(0)
    parts = range(len(col_offsets))

    def tile_copy(e, fcol, part):
        col = pl.multiple_of(fcol * EXPERT_COLS + col_offsets[part], EXPERT_COLS)
        return pltpu.make_async_copy(w_hbm.at[0, e, :, pl.ds(col, EXPERT_COLS)],
                                     land.at[part], sem.at[part])

    @pl.when(jnp.logical_and(f == 0, u == 0))
    def _():
        for part in parts:
            tile_copy(ue_ref[0], 0, part).start()

    def cast_rows(part):
        def body(i, carry):
            r = pl.multiple_of(i * CAST_ROWS, CAST_ROWS)
            wbf_refs[part][pl.ds(r, CAST_ROWS), :] = (
                land[part, pl.ds(r, CAST_ROWS), :].astype(BF16))
            return carry
        lax.fori_loop(0, land.shape[1] // CAST_ROWS, body, 0)

    @pl.when(uf_ref[u] == 1)
    def _():
        for part in parts:
            tile_copy(ue_ref[u], f, part).wait()
            cast_rows(part)
        wraps = nxw_ref[u]

        @pl.when(jnp.logical_not(jnp.logical_and(wraps == 1, f == nf - 1)))
        def _():
            for part in parts:
                tile_copy(nxe_ref[u], f + wraps, part).start()


def _gate_up_kernel(ue_ref, ub_ref, uf_ref, nxe_ref, nxw_ref, na_ref, x_ref, w_hbm, bg_ref,
                    bu_ref, o_ref, land, wgb_ref, wub_ref, sem):
    del ub_ref
    u = pl.program_id(1)
    ff = pl.num_programs(0) * EXPERT_COLS
    _stream_weight_tiles(w_hbm, (0, ff), land, (wgb_ref, wub_ref), sem,
                         ue_ref, uf_ref, nxe_ref, nxw_ref)

    @pl.when(u < na_ref[0])
    def _():
        for c in range(EXPERT_COLS // COL_TILE):
            sl = slice(c * COL_TILE, (c + 1) * COL_TILE)
            gate = jnp.dot(x_ref[...], wgb_ref[:, sl],
                           preferred_element_type=F32) + bg_ref[:, sl]
            up = jnp.dot(x_ref[...], wub_ref[:, sl],
                         preferred_element_type=F32) + bu_ref[:, sl]
            gate = jnp.minimum(gate, SWIGLU_LIMIT)
            up = jnp.clip(up, -SWIGLU_LIMIT, SWIGLU_LIMIT)
            act = gate * jax.nn.sigmoid(SWIGLU_ALPHA * gate) * (up + 1.0)
            o_ref[:, sl] = act.astype(o_ref.dtype)

    @pl.when(u >= na_ref[0])
    def _():
        o_ref[...] = jnp.zeros_like(o_ref)


def _unit_imap(fn):
    return lambda c, u, ue, ub, uf, nxe, nxw, na: fn(c, u, ue, ub)


def _gate_up(xs, w_gate_up, b_gate_up, units):
    n_rows, d = xs.shape
    ne = w_gate_up.shape[1]
    ff = w_gate_up.shape[3] // 2
    nu = n_rows // EXPERT_TILE
    nf = ff // EXPERT_COLS
    bias = b_gate_up.reshape(ne, 1, 2 * ff)
    grid_spec = pltpu.PrefetchScalarGridSpec(
        num_scalar_prefetch=len(units),
        grid=(nf, nu),
        in_specs=[
            pl.BlockSpec((EXPERT_TILE, d), _unit_imap(lambda f, u, ue, ub: (ub[u], 0))),
            pl.BlockSpec(memory_space=pl.ANY),
            pl.BlockSpec((None, 1, EXPERT_COLS),
                         _unit_imap(lambda f, u, ue, ub: (ue[u], 0, f))),
            pl.BlockSpec((None, 1, EXPERT_COLS),
                         _unit_imap(lambda f, u, ue, ub: (ue[u], 0, nf + f))),
        ],
        out_specs=pl.BlockSpec((EXPERT_TILE, EXPERT_COLS),
                               _unit_imap(lambda f, u, ue, ub: (u, f))),
        scratch_shapes=[pltpu.VMEM((2, d, EXPERT_COLS), F32),
                        pltpu.VMEM((d, EXPERT_COLS), BF16),
                        pltpu.VMEM((d, EXPERT_COLS), BF16),
                        pltpu.SemaphoreType.DMA((2,))],
    )
    return pl.pallas_call(
        _gate_up_kernel,
        grid_spec=grid_spec,
        out_shape=jax.ShapeDtypeStruct((n_rows, ff), BF16),
        compiler_params=_params(("arbitrary", "arbitrary"), EXPERT_VMEM_LIMIT),
        name="expert_gate_up",
    )(*units, xs, w_gate_up, bias, bias)


def _down_kernel(ue_ref, ub_ref, uf_ref, nxe_ref, nxw_ref, na_ref, a_ref, w_hbm, b_ref,
                 o_ref, land, wb_ref, sem):
    del ub_ref
    u = pl.program_id(1)
    _stream_weight_tiles(w_hbm, (0,), land, (wb_ref,), sem,
                         ue_ref, uf_ref, nxe_ref, nxw_ref)

    @pl.when(u < na_ref[0])
    def _():
        o_ref[...] = jnp.dot(a_ref[...], wb_ref[...], preferred_element_type=F32) + b_ref[...]

    @pl.when(u >= na_ref[0])
    def _():
        o_ref[...] = jnp.zeros_like(o_ref)


def _down(act, w_down, b_down, units):
    n_rows, ff = act.shape
    ne = w_down.shape[1]
    d = w_down.shape[3]
    nu = n_rows // EXPERT_TILE
    bias = b_down.reshape(ne, 1, d)
    grid_spec = pltpu.PrefetchScalarGridSpec(
        num_scalar_prefetch=len(units),
        grid=(d // EXPERT_COLS, nu),
        in_specs=[
            pl.BlockSpec((EXPERT_TILE, ff), _unit_imap(lambda n, u, ue, ub: (ub[u], 0))),
            pl.BlockSpec(memory_space=pl.ANY),
            pl.BlockSpec((None, 1, EXPERT_COLS),
                         _unit_imap(lambda n, u, ue, ub: (ue[u], 0, n))),
        ],
        out_specs=pl.BlockSpec((EXPERT_TILE, EXPERT_COLS),
                               _unit_imap(lambda n, u, ue, ub: (u, n))),
        scratch_shapes=[pltpu.VMEM((1, ff, EXPERT_COLS), F32),
                        pltpu.VMEM((ff, EXPERT_COLS), BF16),
                        pltpu.SemaphoreType.DMA((1,))],
    )
    return pl.pallas_call(
        _down_kernel,
        grid_spec=grid_spec,
        out_shape=jax.ShapeDtypeStruct((n_rows, d), F32),
        compiler_params=_params(("arbitrary", "arbitrary")),
        name="expert_down",
    )(*units, act, w_down, bias)


def _combine_kernel(pos_ref, nxt_ref, x_ref, w_ref, y_hbm, op_ref, os_ref, buf_ref, sem, *,
                    n_prompt_blocks):
    tb = x_ref.shape[0]
    i = pl.program_id(0)
    nblk = pl.num_programs(0)
    slot = lax.rem(i, 2)

    def row_copy(idx_ref, t, k, s):
        return pltpu.make_async_copy(y_hbm.at[pl.ds(idx_ref[0, 0, t * TOP_K + k], 1), :],
                                     buf_ref.at[s, k, pl.ds(t, 1), :], sem.at[s])

    def request(idx_ref, s):
        def body(t, c):
            for k in range(TOP_K):
                row_copy(idx_ref, t, k, s).start(priority=k % 2)
            return c
        lax.fori_loop(0, tb, body, 0)

    @pl.when(i == 0)
    def _():
        request(pos_ref, slot)

    @pl.when(i + 1 < nblk)
    def _():
        request(nxt_ref, 1 - slot)

    def wait(t, c):
        for k in range(TOP_K):
            row_copy(pos_ref, t, k, slot).wait()
        return c

    lax.fori_loop(0, tb, wait, 0)
    w = w_ref[...]
    acc = w[:, 0:1] * buf_ref[slot, 0]
    for k in range(1, TOP_K):
        acc = acc + w[:, k:k + 1] * buf_ref[slot, k]
    y = x_ref[...] + acc

    @pl.when(i < n_prompt_blocks)
    def _():
        op_ref[...] = y

    @pl.when(i >= n_prompt_blocks)
    def _():
        os_ref[...] = y


def _combine(x1, wts, pos, y_sorted, n_tok, n_seq):
    m, d = x1.shape
    nblk = m // TOK_BLOCK
    npb = n_tok // TOK_BLOCK
    assert n_seq == TOK_BLOCK and npb + 1 == nblk
    posb = pos.reshape(nblk, 1, TOK_BLOCK * TOP_K)
    return pl.pallas_call(
        functools.partial(_combine_kernel, n_prompt_blocks=npb),
        grid=(nblk,),
        in_specs=[pl.BlockSpec((1, 1, TOK_BLOCK * TOP_K), lambda i: (i, 0, 0),
                               memory_space=pltpu.SMEM),
                  pl.BlockSpec((1, 1, TOK_BLOCK * TOP_K),
                               lambda i: (jnp.minimum(i + 1, nblk - 1), 0, 0),
                               memory_space=pltpu.SMEM),
                  pl.BlockSpec((TOK_BLOCK, d), lambda i: (i, 0)),
                  pl.BlockSpec((TOK_BLOCK, wts.shape[1]), lambda i: (i, 0)),
                  pl.BlockSpec(memory_space=pl.ANY)],
        out_specs=[pl.BlockSpec((TOK_BLOCK, d), lambda i: (jnp.minimum(i, npb - 1), 0)),
                   pl.BlockSpec((TOK_BLOCK, d), lambda i: (0, 0))],
        out_shape=[jax.ShapeDtypeStruct((n_tok, d), F32),
                   jax.ShapeDtypeStruct((n_seq, d), F32)],
        scratch_shapes=[pltpu.VMEM((2, TOP_K, TOK_BLOCK, d), F32),
                        pltpu.SemaphoreType.DMA((2,))],
        compiler_params=_params(("arbitrary",)),
        name="combine",
    )(posb, posb, x1, wts, y_sorted)


def _moe(x1, layer_w, n_tok):
    norm_ffn, w_router, b_router, w_gate_up, b_gate_up, w_down, b_down = layer_w
    m = x1.shape[0]
    ne = w_router.shape[1]
    h, eidx, rank, wts, counts = _router(x1, norm_ffn, w_router, b_router)

    tile = EXPERT_TILE
    counts = counts[0]
    padded = (counts + tile - 1) // tile * tile
    pend = jnp.cumsum(padded)
    pstart = pend - padded
    eidx = eidx[:, :TOP_K]
    pos = (pstart[eidx] + rank[:, :TOP_K]).astype(jnp.int32)
    n_rows = (m * TOP_K + ne * (tile - 1) + tile - 1) // tile * tile
    nu = n_rows // tile
    tok = jnp.broadcast_to(jnp.arange(m, dtype=jnp.int32)[:, None], pos.shape)
    row_tok = jnp.zeros((n_rows,), jnp.int32).at[pos.reshape(-1)].set(tok.reshape(-1))
    n_active = (pend[-1] // tile).astype(jnp.int32)
    u = jnp.arange(nu, dtype=jnp.int32)
    unit_blk = jnp.minimum(u, n_active - 1)
    unit_e = jnp.sum((pend[None, :] <= (unit_blk * tile)[:, None]).astype(jnp.int32), axis=1)
    unit_e = jnp.minimum(unit_e, ne - 1)
    prev_e = jnp.concatenate([jnp.full((1,), -1, jnp.int32), unit_e[:-1]])
    first = (u < n_active) & (unit_e != prev_e)
    unit_first = first.astype(jnp.int32)
    later = first[None, :] & (u[None, :] > u[:, None])
    nxt = jnp.min(jnp.where(later, u[None, :], nu), axis=1)
    next_wraps = (nxt >= nu).astype(jnp.int32)
    next_e = jnp.where(nxt >= nu, unit_e[0], unit_e[jnp.minimum(nxt, nu - 1)])
    units = (unit_e, unit_blk, unit_first, next_e.astype(jnp.int32), next_wraps,
             n_active.reshape(1))

    xs = _dispatch(h, row_tok, n_rows)
    act = _gate_up(xs, w_gate_up, b_gate_up, units)
    ys = _down(act, w_down, b_down, units)
    return _combine(x1, wts, pos, ys, n_tok, m - n_tok)


def kernel(x_prompt, x_sample, cache_kv_w128, cache_kv_w512, cache_kv_w2048, state_pool,
           norm_mix, w_in, q_norm, k_norm, w_pool_group, pool_scale, w_branch_attn,
           w_branch_pool, w_out, norm_ffn, w_router, b_router, w_gate_up, b_gate_up,
           w_down, b_down):
    depth = norm_mix.shape[0]
    assert depth == 1 and x_prompt.shape[0] == 1 and x_sample.shape[1] == 1
    n_tok, d_model = x_prompt.shape[1], x_prompt.shape[2]
    n_seq = x_sample.shape[0]
    caches = (cache_kv_w128, cache_kv_w512, cache_kv_w2048)
    n_grp = len(ATTN_GROUPS)
    hw = HEADS * HEAD_DIM
    qkv_w = n_grp * hw
    pool_w = w_pool_group.shape[1] * w_pool_group.shape[2]
    width = w_in.shape[2]
    layer = 0

    xp, xs = x_prompt[0], x_sample[:, 0]

    ones = jnp.ones((width - 2 * qkv_w,), F32)
    colgain = jnp.concatenate([
        jnp.broadcast_to(q_norm[layer][:, None, :], (n_grp, HEADS, HEAD_DIM)).reshape(-1),
        jnp.broadcast_to(k_norm[layer][:, None, :], (n_grp, HEADS, HEAD_DIM)).reshape(-1),
        ones]).reshape(1, width)

    h = _rmsnorm(xp, xs, norm_mix[layer])
    z = _inproj(h, w_in[layer], colgain, 2 * qkv_w, qkv_w + pool_w)

    po, plse, so, slse = [], [], [], []
    for g, (window, dil) in enumerate(ATTN_GROUPS):
        assert window == BAND * dil
        o, lse = _attn_prompt(z, g, dil, n_tok)
        po.append(o)
        plse.append(lse)
        o, lse = _attn_sample(z, caches[g][layer], g, dil, n_tok, n_seq)
        so.append(o)
        slse.append(lse)
    o_attn = _merge_groups(po, plse, so, slse)

    pool_out = _pool_branch(z, state_pool[layer], w_pool_group[layer], pool_scale[layer],
                            n_tok, n_seq, 3 * qkv_w)
    mixed = _mix(o_attn, pool_out, w_branch_attn[layer], w_branch_pool[layer], z,
                 3 * qkv_w + pool_w)
    x1 = _outproj(mixed, w_out[layer], xp, xs)
    y_p, y_s = _moe(x1, (norm_ffn[layer], w_router[layer], b_router[layer],
                         w_gate_up, b_gate_up, w_down, b_down), n_tok)
    y_prompt = y_p[None]
    y_sample = y_s[:, None]

    outs = [y_prompt, y_sample]
    k_all = z[:, qkv_w:2 * qkv_w]
    v_all = z[:, 2 * qkv_w:3 * qkv_w]
    for g, (window, dil) in enumerate(ATTN_GROUPS):
        keep = min(window, n_tok)
        sl = slice(g * hw, (g + 1) * hw)
        kp = k_all[n_tok - keep:n_tok, sl].reshape(keep, HEADS, HEAD_DIM)
        vp = v_all[n_tok - keep:n_tok, sl].reshape(keep, HEADS, HEAD_DIM)
        outs.append(jnp.stack([kp, vp], axis=0)[None, None])
        ks = k_all[n_tok:, sl].reshape(n_seq, 1, HEADS, HEAD_DIM)
        vs = v_all[n_tok:, sl].reshape(n_seq, 1, HEADS, HEAD_DIM)
        outs.append(jnp.stack([ks, vs], axis=1)[None])
    u_all = z[:, 3 * qkv_w:3 * qkv_w + pool_w]
    keep = min(POOL_HALO - 1, n_tok)
    outs.append(u_all[n_tok - keep:n_tok][None, None])
    outs.append(u_all[n_tok:][:, None][None])
    return tuple(outs)
```
